```python
import jax, jax.numpy as jnp
from jax import lax
import numpy as np

D_MODEL = 4096
BATCH = 4
SEQ = 2048
DEPTH = 2
DEC_BATCH = 128
DEC_SEQ = 8
PAST_LEN = 16384
PAGE_SIZE = 128

N_A_LAYERS = DEPTH // 2
N_B_LAYERS = DEPTH - N_A_LAYERS
CONV_WIDTH = 31
CONV_STATE = CONV_WIDTH - 1
D_FF = -(-8 * D_MODEL // (3 * 256)) * 256
N_HEADS = D_MODEL // 64
Q_LORA = 1536
KV_LORA = 512
QK_NOPE = 128
QK_ROPE = 64
QK_HEAD = QK_NOPE + QK_ROPE
V_DIM = 128
ROPE_THETA = 10000.0
LN_EPS = 1e-5
RMS_EPS = 1e-6
ALPHA = (2.0 * DEPTH) ** 0.25
BETA = (8.0 * DEPTH) ** -0.25
Q_BLOCK = 128
SM_SCALE = QK_HEAD ** -0.5

kernel_name = 'yoco_conformer_conv_mla_decoder_step'


def layer_norm(x, g, b):
    xf = x.astype(jnp.float32)
    mu = xf.mean(-1, keepdims=True)
    var = jnp.square(xf - mu).mean(-1, keepdims=True)
    return ((xf - mu) * lax.rsqrt(var + LN_EPS) * g + b).astype(x.dtype)


def rms_norm(x, g):
    xf = x.astype(jnp.float32)
    return (xf * lax.rsqrt(jnp.square(xf).mean(-1, keepdims=True) + RMS_EPS) * g).astype(x.dtype)


def rope(x, pos):
    half = QK_ROPE // 2
    inv = 1.0 / (ROPE_THETA ** (jnp.arange(half, dtype=jnp.float32) / half))
    ang = pos.astype(jnp.float32)[:, None] * inv[None, :]
    shape = (ang.shape[0],) + (1,) * (x.ndim - 3) + (half,)
    cos = jnp.cos(ang).reshape(shape)
    sin = jnp.sin(ang).reshape(shape)
    x1 = x[..., :half].astype(jnp.float32)
    x2 = x[..., half:].astype(jnp.float32)
    return jnp.concatenate([x1 * cos - x2 * sin, x2 * cos + x1 * sin], axis=-1).astype(x.dtype)


def conformer_conv(h, conv_past, w_pw1, b_pw1, w_dw, b_dw, ln_g, ln_b, w_pw2, b_pw2):
    u = h @ w_pw1 + b_pw1
    g = u[..., :D_MODEL] * jax.nn.sigmoid(u[..., D_MODEL:])
    if conv_past is None:
        padded = jnp.pad(g, ((0, 0), (CONV_STATE, 0), (0, 0)))
    else:
        padded = jnp.concatenate([conv_past.astype(g.dtype), g], axis=1)
    conv = lax.conv_general_dilated(padded, w_dw[:, None, :].astype(g.dtype), window_strides=(1,),
                                    padding='VALID', dimension_numbers=('NWC', 'WIO', 'NWC'),
                                    feature_group_count=D_MODEL)
    z = jax.nn.silu(layer_norm(conv + b_dw, ln_g, ln_b))
    return z @ w_pw2 + b_pw2, padded[:, -CONV_STATE:]


def swiglu(h, w_gate, w_up, w_down):
    return (jax.nn.silu(h @ w_gate) * (h @ w_up)) @ w_down


def shared_kv(h, pos, w_dkv, kv_norm, w_kr):
    c = rms_norm(h @ w_dkv, kv_norm)
    kr = rope(h @ w_kr, pos)
    return c, kr


def mla_queries(h, pos, w_dq, q_norm, w_uq):
    cq = rms_norm(h @ w_dq, q_norm)
    q = jnp.einsum('bsq,qhd->bshd', cq, w_uq)
    return q[..., :QK_NOPE], rope(q[..., QK_NOPE:], pos)


def mla_prompt_attend(qn, qr, c, kr, w_uk, w_uv):
    b, s = qn.shape[0], qn.shape[1]
    nb = s // Q_BLOCK
    kn = jnp.einsum('bsc,chd->bshd', c, w_uk)
    v = jnp.einsum('bsc,chd->bshd', c, w_uv)
    qn_b = qn.reshape(b, nb, Q_BLOCK, N_HEADS, QK_NOPE).swapaxes(0, 1)
    qr_b = qr.reshape(b, nb, Q_BLOCK, N_HEADS, QK_ROPE).swapaxes(0, 1)
    kpos = jnp.arange(s)

    def block(args):
        i, qn_i, qr_i = args
        sc = (jnp.einsum('bqhd,bkhd->bhqk', qn_i, kn) + jnp.einsum('bqhr,bkr->bhqk', qr_i, kr)).astype(jnp.float32) * SM_SCALE
        qpos = i * Q_BLOCK + jnp.arange(Q_BLOCK)
        sc = jnp.where(kpos[None, :] <= qpos[:, None], sc, -jnp.inf)
        p = jax.nn.softmax(sc, axis=-1).astype(v.dtype)
        return jnp.einsum('bhqk,bkhd->bqhd', p, v)

    o = lax.map(block, (jnp.arange(nb), qn_b, qr_b))
    return o.swapaxes(0, 1).reshape(b, s, N_HEADS, V_DIM)


def mla_sample_attend(qn, qr, c_new, kr_new, cache_ckv, cache_krope, page_table, w_uk, w_uv):
    t = qn.shape[1]
    past = page_table.shape[1] * PAGE_SIZE
    q_lat = jnp.einsum('bthd,chd->bthc', qn, w_uk)
    kpos = jnp.arange(past + t)
    mask = kpos[None, :] <= (past + jnp.arange(t))[:, None]

    def one(args):
        ql, qr_s, cn, kn, pt = args
        c_all = jnp.concatenate([cache_ckv[pt].reshape(past, KV_LORA).astype(cn.dtype), cn], axis=0)
        r_all = jnp.concatenate([cache_krope[pt].reshape(past, QK_ROPE).astype(kn.dtype), kn], axis=0)
        sc = (jnp.einsum('thc,kc->htk', ql, c_all) + jnp.einsum('thr,kr->htk', qr_s, r_all)).astype(jnp.float32) * SM_SCALE
        sc = jnp.where(mask[None], sc, -jnp.inf)
        p = jax.nn.softmax(sc, axis=-1).astype(c_all.dtype)
        return jnp.einsum('htk,kc->thc', p, c_all)

    o_lat = lax.map(one, (q_lat, qr, c_new, kr_new, page_table))
    return jnp.einsum('bthc,chd->bthd', o_lat, w_uv)


def trunk(x, pos, conv_states, attend, ln_g, ln_b,
          conv_w_pw1, conv_b_pw1, conv_w_dw, conv_b_dw, conv_ln_g, conv_ln_b, conv_w_pw2, conv_b_pw2,
          mla_w_dq, mla_q_norm, mla_w_uq, mla_w_o, kv_w_dkv, kv_norm, kv_w_kr,
          ffn_w_gate, ffn_w_up, ffn_w_down):
    new_conv = []
    c = kr = None
    for l in range(DEPTH):
        if l < N_A_LAYERS:
            past = None if conv_states is None else conv_states[l]
            out, st = conformer_conv(x, past, conv_w_pw1[l], conv_b_pw1[l], conv_w_dw[l], conv_b_dw[l],
                                     conv_ln_g[l], conv_ln_b[l], conv_w_pw2[l], conv_b_pw2[l])
            new_conv.append(st)
        else:
            if l == N_A_LAYERS:
                c, kr = shared_kv(x, pos, kv_w_dkv, kv_norm, kv_w_kr)
            j = l - N_A_LAYERS
            qn, qr = mla_queries(x, pos, mla_w_dq[j], mla_q_norm[j], mla_w_uq[j])
            o = attend(qn, qr, c, kr)
            out = jnp.einsum('bshd,hdm->bsm', o, mla_w_o[j])
        x = layer_norm(ALPHA * x + out, ln_g[l, 0], ln_b[l, 0])
        x = layer_norm(ALPHA * x + swiglu(x, ffn_w_gate[l], ffn_w_up[l], ffn_w_down[l]), ln_g[l, 1], ln_b[l, 1])
    return x, jnp.stack(new_conv), c, kr


def setup_inputs(seed: int = 0) -> dict:
    key = jax.random.key(seed)
    ks = iter(jax.random.split(key, 40))
    f32 = jnp.float32

    def nrm(shape, scale):
        return jax.random.normal(next(ks), shape, f32) * scale

    D, H = D_MODEL, N_HEADS
    n_pages = PAST_LEN // PAGE_SIZE
    n_phys = (DEC_BATCH * n_pages * 5 + 3) // 4
    page_table = jax.random.permutation(next(ks), n_phys)[: DEC_BATCH * n_pages].reshape(DEC_BATCH, n_pages).astype(jnp.int32)
    return {
        'x_prompt': nrm((BATCH, SEQ, D), 1.0),
        'x_sample': nrm((DEC_BATCH, DEC_SEQ, D), 1.0),
        'state_conv': nrm((N_A_LAYERS, DEC_BATCH, CONV_STATE, D), 0.5),
        'cache_ckv': nrm((n_phys, PAGE_SIZE, KV_LORA), 1.0),
        'cache_krope': nrm((n_phys, PAGE_SIZE, QK_ROPE), 1.0),
        'page_table': page_table,
        'ln_g': 1.0 + nrm((DEPTH, 2, D), 0.02),
        'ln_b': nrm((DEPTH, 2, D), 0.02),
        'conv_w_pw1': nrm((N_A_LAYERS, D, 2 * D), D ** -0.5),
        'conv_b_pw1': nrm((N_A_LAYERS, 2 * D), 0.02),
        'conv_w_dw': nrm((N_A_LAYERS, CONV_WIDTH, D), CONV_WIDTH ** -0.5),
        'conv_b_dw': nrm((N_A_LAYERS, D), 0.02),
        'conv_ln_g': 1.0 + nrm((N_A_LAYERS, D), 0.02),
        'conv_ln_b': nrm((N_A_LAYERS, D), 0.02),
        'conv_w_pw2': nrm((N_A_LAYERS, D, D), BETA * D ** -0.5),
        'conv_b_pw2': nrm((N_A_LAYERS, D), 0.02),
        'mla_w_dq': nrm((N_B_LAYERS, D, Q_LORA), D ** -0.5),
        'mla_q_norm': 1.0 + nrm((N_B_LAYERS, Q_LORA), 0.02),
        'mla_w_uq': nrm((N_B_LAYERS, Q_LORA, H, QK_HEAD), Q_LORA ** -0.5),
        'mla_w_o': nrm((N_B_LAYERS, H, V_DIM, D), BETA * (H * V_DIM) ** -0.5),
        'kv_w_dkv': nrm((D, KV_LORA), D ** -0.5),
        'kv_norm': 1.0 + nrm((KV_LORA,), 0.02),
        'kv_w_kr': nrm((D, QK_ROPE), D ** -0.5),
        'kv_w_uk': nrm((KV_LORA, H, QK_NOPE), KV_LORA ** -0.5),
        'kv_w_uv': nrm((KV_LORA, H, V_DIM), KV_LORA ** -0.5),
        'ffn_w_gate': nrm((DEPTH, D, D_FF), D ** -0.5),
        'ffn_w_up': nrm((DEPTH, D, D_FF), D ** -0.5),
        'ffn_w_down': nrm((DEPTH, D_FF, D), BETA * D_FF ** -0.5),
    }


def reference(x_prompt, x_sample, state_conv, cache_ckv, cache_krope, page_table, ln_g, ln_b,
              conv_w_pw1, conv_b_pw1, conv_w_dw, conv_b_dw, conv_ln_g, conv_ln_b, conv_w_pw2, conv_b_pw2,
              mla_w_dq, mla_q_norm, mla_w_uq, mla_w_o, kv_w_dkv, kv_norm, kv_w_kr, kv_w_uk, kv_w_uv,
              ffn_w_gate, ffn_w_up, ffn_w_down):
    params = (ln_g, ln_b, conv_w_pw1, conv_b_pw1, conv_w_dw, conv_b_dw, conv_ln_g, conv_ln_b, conv_w_pw2, conv_b_pw2,
              mla_w_dq, mla_q_norm, mla_w_uq, mla_w_o, kv_w_dkv, kv_norm, kv_w_kr, ffn_w_gate, ffn_w_up, ffn_w_down)

    prompt_pos = jnp.arange(x_prompt.shape[1])
    past_len = page_table.shape[1] * PAGE_SIZE
    sample_pos = past_len + jnp.arange(x_sample.shape[1])

    def prompt_attend(qn, qr, c, kr):
        return mla_prompt_attend(qn, qr, c, kr, kv_w_uk, kv_w_uv)

    def sample_attend(qn, qr, c, kr):
        return mla_sample_attend(qn, qr, c, kr, cache_ckv, cache_krope, page_table, kv_w_uk, kv_w_uv)

    y_prompt, conv_prompt, ckv_prompt, krope_prompt = trunk(x_prompt, prompt_pos, None, prompt_attend, *params)
    y_sample, conv_sample, ckv_sample, krope_sample = trunk(x_sample, sample_pos, state_conv, sample_attend, *params)
    return (y_prompt, y_sample, conv_prompt, conv_sample, ckv_prompt, ckv_sample, krope_prompt, krope_sample)
```

```python
import functools

import jax
import jax.numpy as jnp
from jax import lax
from jax.experimental import pallas as pl
from jax.experimental.pallas import tpu as pltpu

F32 = jnp.float32
BF16 = jnp.bfloat16

LN_EPS = 1e-5
RMS_EPS = 1e-6
ROPE_THETA = 10000.0
QK_NOPE = 128
QK_ROPE = 64
V_DIM = 128
HEAD_EXT = 256
LANES = 128
V7X_VMEM_LIMIT = 56 * 1024 * 1024

_NT = (((1,), (1,)), ((), ()))


def _params(*sem):
    return pltpu.CompilerParams(dimension_semantics=sem, vmem_limit_bytes=V7X_VMEM_LIMIT)


def _blk(dim, pref, unit=LANES):
    if dim <= pref:
        return dim
    b = (pref // unit) * unit
    while b > unit and dim % b:
        b -= unit
    assert dim % b == 0, (dim, pref, unit)
    return b


def _layer_norm(x, g, b):
    mu = jnp.mean(x, axis=-1, keepdims=True)
    xc = x - mu
    var = jnp.mean(xc * xc, axis=-1, keepdims=True)
    return xc * lax.rsqrt(var + LN_EPS) * g + b


def _glu_kernel(x_ref, wa_ref, wb_ref, ba_ref, bb_ref, o_ref):
    x = x_ref[...]
    a = jnp.dot(x, wa_ref[...], preferred_element_type=F32) + ba_ref[...]
    b = jnp.dot(x, wb_ref[...], preferred_element_type=F32) + bb_ref[...]
    o_ref[...] = a * jax.nn.sigmoid(b)


def _pw1_glu(xb, w, bias):
    m, k = xb.shape
    d = w.shape[1] // 2
    bm, bn = _blk(m, 1024, 8), _blk(d, 512)
    nj = d // bn
    return pl.pallas_call(
        _glu_kernel,
        grid=(m // bm, nj),
        in_specs=[
            pl.BlockSpec((bm, k), lambda i, j: (i, 0)),
            pl.BlockSpec((k, bn), lambda i, j: (0, j)),
            pl.BlockSpec((k, bn), lambda i, j: (0, j + nj)),
            pl.BlockSpec((1, bn), lambda i, j: (0, j)),
            pl.BlockSpec((1, bn), lambda i, j: (0, j + nj)),
        ],
        out_specs=pl.BlockSpec((bm, bn), lambda i, j: (i, j)),
        out_shape=jax.ShapeDtypeStruct((m, d), F32),
        compiler_params=_params("parallel", "arbitrary"),
        name="pw1_glu",
    )(xb, w, w, bias, bias)


def _conv_prompt_kernel(g_ref, halo_ref, w_ref, bdw_ref, lg_ref, lb_ref, z_ref, win_ref, y_ref,
                        *, tt, kw, halo, rt, cw):
    i = pl.program_id(1)
    d = y_ref.shape[1]
    keep = jnp.where(i == 0, 0.0, 1.0).astype(F32)
    win_ref[0:halo, :] = halo_ref[0] * keep
    win_ref[halo:halo + tt, :] = g_ref[0]
    base = halo - (kw - 1)

    def chunk(c, carry):
        off = pl.multiple_of(c * cw, cw)
        for r in range(tt // rt):
            acc = jnp.zeros((rt, cw), F32)
            for k in range(kw):
                acc = acc + w_ref[k:k + 1, pl.ds(off, cw)] * win_ref[pl.ds(r * rt + k + base, rt), pl.ds(off, cw)]
            y_ref[r * rt:(r + 1) * rt, pl.ds(off, cw)] = acc + bdw_ref[:, pl.ds(off, cw)]
        return carry

    lax.fori_loop(0, d // cw, chunk, 0)
    y = _layer_norm(y_ref[...], lg_ref[...], lb_ref[...])
    z_ref[0] = (y * jax.nn.sigmoid(y)).astype(BF16)


def _conv_prompt(g3, w_dw, b_dw, ln_g, ln_b):
    b, s, d = g3.shape
    kw = w_dw.shape[0]
    halo = 32
    assert kw - 1 <= halo
    tt = _blk(s, 256, halo)
    rt, cw = _blk(tt, 32, 8), _blk(d, 512)
    hb = tt // halo
    kern = functools.partial(_conv_prompt_kernel, tt=tt, kw=kw, halo=halo, rt=rt, cw=cw)
    return pl.pallas_call(
        kern,
        grid=(b, s // tt),
        in_specs=[
            pl.BlockSpec((1, tt, d), lambda bi, i: (bi, i, 0)),
            pl.BlockSpec((1, halo, d), lambda bi, i: (bi, jnp.maximum(i * hb - 1, 0), 0)),
            pl.BlockSpec((kw, d), lambda bi, i: (0, 0)),
            pl.BlockSpec((1, d), lambda bi, i: (0, 0)),
            pl.BlockSpec((1, d), lambda bi, i: (0, 0)),
            pl.BlockSpec((1, d), lambda bi, i: (0, 0)),
        ],
        out_specs=pl.BlockSpec((1, tt, d), lambda bi, i: (bi, i, 0)),
        out_shape=jax.ShapeDtypeStruct((b, s, d), BF16),
        scratch_shapes=[pltpu.VMEM((halo + tt, d), F32), pltpu.VMEM((tt, d), F32)],
        compiler_params=_params("parallel", "arbitrary"),
        name="conv_prompt",
    )(g3, g3, w_dw, b_dw, ln_g, ln_b)


def _conv_sample_kernel(st_ref, g_ref, w_ref, bdw_ref, lg_ref, lb_ref, z_ref, ns_ref, pad_ref, *, t, kw):
    cs = kw - 1
    pad_ref[:, 0:cs, :] = st_ref[...]
    pad_ref[:, cs:cs + t, :] = g_ref[...]
    acc = jnp.zeros(g_ref.shape, F32)
    for k in range(kw):
        acc = acc + w_ref[k:k + 1, :][None] * pad_ref[:, k:k + t, :]
    y = _layer_norm(acc + bdw_ref[...][None], lg_ref[...][None], lb_ref[...][None])
    z_ref[...] = (y * jax.nn.sigmoid(y)).astype(BF16)
    ns_ref[...] = pad_ref[:, t:t + cs, :]


def _conv_sample(state, g3, w_dw, b_dw, ln_g, ln_b):
    bd, t, d = g3.shape
    kw = w_dw.shape[0]
    cs = kw - 1
    bb = _blk(bd, 8, 1)
    kern = functools.partial(_conv_sample_kernel, t=t, kw=kw)
    return pl.pallas_call(
        kern,
        grid=(bd // bb,),
        in_specs=[
            pl.BlockSpec((bb, cs, d), lambda i: (i, 0, 0)),
            pl.BlockSpec((bb, t, d), lambda i: (i, 0, 0)),
            pl.BlockSpec((kw, d), lambda i: (0, 0)),
            pl.BlockSpec((1, d), lambda i: (0, 0)),
            pl.BlockSpec((1, d), lambda i: (0, 0)),
            pl.BlockSpec((1, d), lambda i: (0, 0)),
        ],
        out_specs=[pl.BlockSpec((bb, t, d), lambda i: (i, 0, 0)),
                   pl.BlockSpec((bb, cs, d), lambda i: (i, 0, 0))],
        out_shape=[jax.ShapeDtypeStruct((bd, t, d), BF16), jax.ShapeDtypeStruct((bd, cs, d), F32)],
        scratch_shapes=[pltpu.VMEM((bb, cs + t, d), F32)],
        compiler_params=_params("parallel"),
        name="conv_sample",
    )(state, g3, w_dw, b_dw, ln_g, ln_b)


def _mm_res_kernel(*refs, alpha, has_bias):
    if has_bias:
        x_ref, w_ref, r_ref, b_ref, o_ref = refs
    else:
        x_ref, w_ref, r_ref, o_ref = refs
        b_ref = None
    k = pl.program_id(2)

    @pl.when(k == 0)
    def _():
        init = alpha * r_ref[...]
        if has_bias:
            init = init + b_ref[...]
        o_ref[...] = init

    o_ref[...] += jnp.dot(x_ref[...], w_ref[...], preferred_element_type=F32)


def _mm_residual(xb, w, resid, bias, alpha, tk_pref):
    m, k = xb.shape
    n = w.shape[1]
    bm, bn, tk = _blk(m, 1024, 8), _blk(n, 512), _blk(k, tk_pref)
    has_bias = bias is not None
    in_specs = [
        pl.BlockSpec((bm, tk), lambda i, j, kk: (i, kk)),
        pl.BlockSpec((tk, bn), lambda i, j, kk: (kk, j)),
        pl.BlockSpec((bm, bn), lambda i, j, kk: (i, j)),
    ]
    args = [xb, w, resid]
    if has_bias:
        in_specs.append(pl.BlockSpec((1, bn), lambda i, j, kk: (0, j)))
        args.append(bias)
    return pl.pallas_call(
        functools.partial(_mm_res_kernel, alpha=alpha, has_bias=has_bias),
        grid=(m // bm, n // bn, k // tk),
        in_specs=in_specs,
        out_specs=pl.BlockSpec((bm, bn), lambda i, j, kk: (i, j)),
        out_shape=jax.ShapeDtypeStruct((m, n), F32),
        compiler_params=_params("parallel", "parallel", "arbitrary"),
        name="mm_residual",
    )(*args)


def _ln_kernel(x_ref, g_ref, b_ref, o_ref, ob_ref):
    y = _layer_norm(x_ref[...], g_ref[...], b_ref[...])
    o_ref[...] = y
    ob_ref[...] = y.astype(BF16)


def _ln(x, g, b):
    m, d = x.shape
    bm = _blk(m, 256, 8)
    return pl.pallas_call(
        _ln_kernel,
        grid=(m // bm,),
        in_specs=[pl.BlockSpec((bm, d), lambda i: (i, 0)),
                  pl.BlockSpec((1, d), lambda i: (0, 0)),
                  pl.BlockSpec((1, d), lambda i: (0, 0))],
        out_specs=[pl.BlockSpec((bm, d), lambda i: (i, 0)), pl.BlockSpec((bm, d), lambda i: (i, 0))],
        out_shape=[jax.ShapeDtypeStruct((m, d), F32), jax.ShapeDtypeStruct((m, d), BF16)],
        compiler_params=_params("parallel"),
        name="layer_norm",
    )(x, g, b)


def _gate_up_kernel(x_ref, wg_ref, wu_ref, o_ref):
    x = x_ref[...]
    a = jnp.dot(x, wg_ref[...], preferred_element_type=F32)
    b = jnp.dot(x, wu_ref[...], preferred_element_type=F32)
    o_ref[...] = (a * jax.nn.sigmoid(a) * b).astype(BF16)


def _gate_up(xb, wg, wu):
    m, k = xb.shape
    n = wg.shape[1]
    bm, bn = _blk(m, 1024, 8), _blk(n, 512)
    return pl.pallas_call(
        _gate_up_kernel,
        grid=(m // bm, n // bn),
        in_specs=[pl.BlockSpec((bm, k), lambda i, j: (i, 0)),
                  pl.BlockSpec((k, bn), lambda i, j: (0, j)),
                  pl.BlockSpec((k, bn), lambda i, j: (0, j))],
        out_specs=pl.BlockSpec((bm, bn), lambda i, j: (i, j)),
        out_shape=jax.ShapeDtypeStruct((m, n), BF16),
        compiler_params=_params("parallel", "arbitrary"),
        name="ffn_gate_up",
    )(xb, wg, wu)


def _dq_kernel(x_ref, w_ref, g_ref, o_ref):
    y = jnp.dot(x_ref[...], w_ref[...], preferred_element_type=F32)
    ms = jnp.mean(y * y, axis=-1, keepdims=True)
    o_ref[...] = (y * lax.rsqrt(ms + RMS_EPS) * g_ref[...]).astype(BF16)


def _dq(xb, w, g):
    m, k = xb.shape
    n = w.shape[1]
    bm = _blk(m, 512, 8)
    return pl.pallas_call(
        _dq_kernel,
        grid=(m // bm,),
        in_specs=[pl.BlockSpec((bm, k), lambda i: (i, 0)),
                  pl.BlockSpec((k, n), lambda i: (0, 0)),
                  pl.BlockSpec((1, n), lambda i: (0, 0))],
        out_specs=pl.BlockSpec((bm, n), lambda i: (i, 0)),
        out_shape=jax.ShapeDtypeStruct((m, n), BF16),
        compiler_params=_params("parallel"),
        name="mla_dq",
    )(xb, w, g)


def _kv_kernel(x_ref, w_ref, g_ref, cs_ref, c_ref, kr_ref, *, lora):
    y = jnp.dot(x_ref[...], w_ref[...], preferred_element_type=F32)
    c = y[:, :lora]
    ms = jnp.mean(c * c, axis=-1, keepdims=True)
    c_ref[...] = c * lax.rsqrt(ms + RMS_EPS) * g_ref[...]
    t = y[:, lora:lora + 2 * QK_ROPE] * cs_ref[...]
    kr_ref[...] = t[:, :QK_ROPE] + t[:, QK_ROPE:]


def _shared_kv(xb, w, g, cs):
    m, k = xb.shape
    n = w.shape[1]
    lora = n - 2 * QK_ROPE
    bm = _blk(m, 512, 8)
    return pl.pallas_call(
        functools.partial(_kv_kernel, lora=lora),
        grid=(m // bm,),
        in_specs=[pl.BlockSpec((bm, k), lambda i: (i, 0)),
                  pl.BlockSpec((k, n), lambda i: (0, 0)),
                  pl.BlockSpec((1, lora), lambda i: (0, 0)),
                  pl.BlockSpec((bm, 2 * QK_ROPE), lambda i: (i, 0))],
        out_specs=[pl.BlockSpec((bm, lora), lambda i: (i, 0)),
                   pl.BlockSpec((bm, QK_ROPE), lambda i: (i, 0))],
        out_shape=[jax.ShapeDtypeStruct((m, lora), F32), jax.ShapeDtypeStruct((m, QK_ROPE), F32)],
        compiler_params=_params("parallel"),
        name="mla_shared_kv",
    )(xb, w, g, cs)


def _uq_kernel(x_ref, w_ref, cs_ref, o_ref, *, hb, scale):
    y = jnp.dot(x_ref[...], w_ref[...], preferred_element_type=F32)
    cs = cs_ref[...]
    for h in range(hb):
        lo = h * HEAD_EXT
        o_ref[:, lo:lo + QK_NOPE] = (y[:, lo:lo + QK_NOPE] * scale).astype(BF16)
        t = y[:, lo + QK_NOPE:lo + HEAD_EXT] * cs
        r = (t + pltpu.roll(t, QK_ROPE, 1)) * scale
        o_ref[:, lo + QK_NOPE:lo + HEAD_EXT] = r.astype(BF16)


def _uq(cq, w_ext, cs, scale):
    m, k = cq.shape
    n = w_ext.shape[1]
    bm, bn = _blk(m, 1024, 8), _blk(n, 1024, HEAD_EXT)
    return pl.pallas_call(
        functools.partial(_uq_kernel, hb=bn // HEAD_EXT, scale=scale),
        grid=(m // bm, n // bn),
        in_specs=[pl.BlockSpec((bm, k), lambda i, j: (i, 0)),
                  pl.BlockSpec((k, bn), lambda i, j: (0, j)),
                  pl.BlockSpec((bm, 2 * QK_ROPE), lambda i, j: (i, 0))],
        out_specs=pl.BlockSpec((bm, bn), lambda i, j: (i, j)),
        out_shape=jax.ShapeDtypeStruct((m, n), BF16),
        compiler_params=_params("parallel", "arbitrary"),
        name="mla_uq",
    )(cq, w_ext, cs)


def _prompt_attn_kernel(q_ref, c_ref, krz_ref, wuk_ref, wuvt_ref, o_ref, kext_ref, vt_ref, *, s, tq):
    c = c_ref[...]
    kext_ref[:, :QK_NOPE] = jnp.dot(c, wuk_ref[...], preferred_element_type=F32).astype(BF16)
    kext_ref[:, QK_NOPE:] = krz_ref[...]
    vt_ref[...] = lax.dot_general(wuvt_ref[...], c, _NT, preferred_element_type=F32).astype(BF16)
    for qi in range(s // tq):
        q = q_ref[qi * tq:(qi + 1) * tq, :]
        m = jnp.full((1, tq), -jnp.inf, F32)
        l = jnp.zeros((1, tq), F32)
        acc = jnp.zeros((V_DIM, tq), F32)
        for kj in range(qi + 1):
            k = kext_ref[kj * tq:(kj + 1) * tq, :]
            st = lax.dot_general(k, q, _NT, preferred_element_type=F32)
            if kj == qi:
                key = lax.broadcasted_iota(jnp.int32, (tq, tq), 0)
                qry = lax.broadcasted_iota(jnp.int32, (tq, tq), 1)
                st = jnp.where(key <= qry, st, -jnp.inf)
            m_new = jnp.maximum(m, jnp.max(st, axis=0, keepdims=True))
            p = jnp.exp(st - m_new)
            a = jnp.exp(m - m_new)
            l = a * l + jnp.sum(p, axis=0, keepdims=True)
            acc = a * acc + jnp.dot(vt_ref[:, kj * tq:(kj + 1) * tq], p.astype(BF16),
                                    preferred_element_type=F32)
            m = m_new
        o_ref[qi * tq:(qi + 1) * tq, :] = (acc / l).T.astype(BF16)


def _prompt_attn(q_ext, cb, krz, wuk2d, wuvt, nb, s):
    h = wuk2d.shape[1] // QK_NOPE
    lora = cb.shape[1]
    tq = _blk(s, 512)
    return pl.pallas_call(
        functools.partial(_prompt_attn_kernel, s=s, tq=tq),
        grid=(nb, h),
        in_specs=[pl.BlockSpec((s, HEAD_EXT), lambda b, hh: (b, hh)),
                  pl.BlockSpec((s, lora), lambda b, hh: (b, 0)),
                  pl.BlockSpec((s, LANES), lambda b, hh: (b, 0)),
                  pl.BlockSpec((lora, QK_NOPE), lambda b, hh: (0, hh)),
                  pl.BlockSpec((V_DIM, lora), lambda b, hh: (hh, 0))],
        out_specs=pl.BlockSpec((s, V_DIM), lambda b, hh: (b, hh)),
        out_shape=jax.ShapeDtypeStruct((nb * s, h * V_DIM), BF16),
        scratch_shapes=[pltpu.VMEM((s, HEAD_EXT), BF16), pltpu.VMEM((V_DIM, s), BF16)],
        compiler_params=_params("parallel", "arbitrary"),
        name="mla_prompt_attn",
    )(q_ext, cb, krz, wuk2d, wuvt)


def _qlat_kernel(q_ref, wuk_ref, ql_ref, qr_ref):
    q = q_ref[...]
    ql_ref[...] = lax.dot_general(q[:, :QK_NOPE], wuk_ref[...], _NT, preferred_element_type=F32).astype(BF16)
    qr_ref[...] = q[:, QK_NOPE:]


def _q_latent(q_ext, wuk2d, row_block, ms):
    lora = wuk2d.shape[0]
    h = wuk2d.shape[1] // QK_NOPE
    return pl.pallas_call(
        _qlat_kernel,
        grid=(h,),
        in_specs=[pl.BlockSpec((ms, HEAD_EXT), lambda hh: (row_block, hh)),
                  pl.BlockSpec((lora, QK_NOPE), lambda hh: (0, hh))],
        out_specs=[pl.BlockSpec((ms, lora), lambda hh: (0, hh)),
                   pl.BlockSpec((ms, LANES), lambda hh: (0, hh))],
        out_shape=[jax.ShapeDtypeStruct((ms, h * lora), BF16), jax.ShapeDtypeStruct((ms, h * LANES), BF16)],
        compiler_params=_params("parallel"),
        name="mla_q_latent",
    )(q_ext, wuk2d)


def _sample_attn_kernel(pt_ref, ql_ref, qr_ref, cn_ref, rn_ref, *rest, pg, nsteps, ps, heads, tks):
    del pt_ref
    page_c, page_r = rest[:pg], rest[pg:2 * pg]
    o_ref, kc_ref, kr_ref, m_ref, l_ref, acc_ref = rest[2 * pg:]
    step = pl.program_id(1)
    ql = ql_ref[0]
    qr = qr_ref[0]
    rows = ql.shape[0]

    @pl.when(step == 0)
    def _():
        cn = cn_ref[0]
        rn = rn_ref[0]
        s = (lax.dot_general(ql, cn, _NT, preferred_element_type=F32)
             + lax.dot_general(qr, rn, _NT, preferred_element_type=F32))
        nk = s.shape[1]
        row = lax.broadcasted_iota(jnp.int32, (rows, nk), 0)
        col = lax.broadcasted_iota(jnp.int32, (rows, nk), 1)
        s = jnp.where(col * heads <= row, s, -jnp.inf)
        m = jnp.max(s, axis=1, keepdims=True)
        p = jnp.exp(s - m)
        m_ref[...] = jnp.broadcast_to(m, m_ref.shape)
        l_ref[...] = jnp.broadcast_to(jnp.sum(p, axis=1, keepdims=True), l_ref.shape)
        acc_ref[...] = jnp.dot(p.astype(BF16), cn, preferred_element_type=F32)
        kr_ref[:, QK_ROPE:] = jnp.zeros((kr_ref.shape[0], LANES - QK_ROPE), BF16)

    for i in range(pg):
        kc_ref[i * ps:(i + 1) * ps, :] = page_c[i][0].astype(BF16)
        kr_ref[i * ps:(i + 1) * ps, :QK_ROPE] = page_r[i][0].astype(BF16)

    for ch in range(pg * ps // tks):
        kc = kc_ref[ch * tks:(ch + 1) * tks, :]
        kr = kr_ref[ch * tks:(ch + 1) * tks, :]
        s = (lax.dot_general(ql, kc, _NT, preferred_element_type=F32)
             + lax.dot_general(qr, kr, _NT, preferred_element_type=F32))
        m_prev = m_ref[...]
        m_new = jnp.maximum(m_prev, jnp.max(s, axis=1, keepdims=True))
        a = jnp.exp(m_prev - m_new)
        p = jnp.exp(s - m_new[:, :1])
        l_ref[...] = a * l_ref[...] + jnp.sum(p, axis=1, keepdims=True)
        acc_ref[...] = acc_ref[...] * a[:, :1] + jnp.dot(p.astype(BF16), kc, preferred_element_type=F32)
        m_ref[...] = m_new

    @pl.when(step == nsteps - 1)
    def _():
        o_ref[0] = (acc_ref[...] / l_ref[...][:, :1]).astype(BF16)


def _sample_attn(page_table, ql3, qr3, cn_pad, rn_pad, cache_ckv, cache_krope, heads):
    bd, rows, lora = ql3.shape
    npages = page_table.shape[1]
    ps = cache_ckv.shape[1]
    pg = _blk(npages, 16, 1)
    nsteps = npages // pg
    tks = _blk(pg * ps, 1024)
    nk = cn_pad.shape[1]
    pt_flat = page_table.reshape(-1)

    def page_spec(i, width):
        return pl.BlockSpec((1, ps, width), lambda b, st, pt: (pt[b * npages + st * pg + i], 0, 0))

    in_specs = [pl.BlockSpec((1, rows, lora), lambda b, st, pt: (b, 0, 0)),
                pl.BlockSpec((1, rows, LANES), lambda b, st, pt: (b, 0, 0)),
                pl.BlockSpec((1, nk, lora), lambda b, st, pt: (b, 0, 0)),
                pl.BlockSpec((1, nk, LANES), lambda b, st, pt: (b, 0, 0))]
    in_specs += [page_spec(i, lora) for i in range(pg)]
    in_specs += [page_spec(i, QK_ROPE) for i in range(pg)]
    kern = functools.partial(_sample_attn_kernel, pg=pg, nsteps=nsteps, ps=ps, heads=heads, tks=tks)
    return pl.pallas_call(
        kern,
        grid_spec=pltpu.PrefetchScalarGridSpec(
            num_scalar_prefetch=1,
            grid=(bd, nsteps),
            in_specs=in_specs,
            out_specs=pl.BlockSpec((1, rows, lora), lambda b, st, pt: (b, 0, 0)),
            scratch_shapes=[pltpu.VMEM((pg * ps, lora), BF16), pltpu.VMEM((pg * ps, LANES), BF16),
                            pltpu.VMEM((rows, LANES), F32), pltpu.VMEM((rows, LANES), F32),
                            pltpu.VMEM((rows, lora), F32)]),
        out_shape=jax.ShapeDtypeStruct((bd, rows, lora), BF16),
        compiler_params=_params("parallel", "arbitrary"),
        name="mla_sample_attn",
    )(pt_flat, ql3, qr3, cn_pad, rn_pad, *([cache_ckv] * pg), *([cache_krope] * pg))


def _uv_kernel(o_ref, w_ref, y_ref):
    y_ref[...] = jnp.dot(o_ref[...], w_ref[...], preferred_element_type=F32).astype(BF16)


def _latent_to_v(olat2d, wuv2d):
    ms = olat2d.shape[0]
    lora = wuv2d.shape[0]
    h = wuv2d.shape[1] // V_DIM
    return pl.pallas_call(
        _uv_kernel,
        grid=(h,),
        in_specs=[pl.BlockSpec((ms, lora), lambda hh: (0, hh)),
                  pl.BlockSpec((lora, V_DIM), lambda hh: (0, hh))],
        out_specs=pl.BlockSpec((ms, V_DIM), lambda hh: (0, hh)),
        out_shape=jax.ShapeDtypeStruct((ms, h * V_DIM), BF16),
        compiler_params=_params("parallel"),
        name="mla_latent_to_v",
    )(olat2d, wuv2d)


def _rope_table(pos):
    half = QK_ROPE // 2
    inv = 1.0 / (ROPE_THETA ** (jnp.arange(half, dtype=F32) / half))
    ang = pos.astype(F32)[:, None] * inv[None, :]
    cos, sin = jnp.cos(ang), jnp.sin(ang)
    return jnp.concatenate([cos, cos, -sin, sin], axis=1)


def _swap_halves(w):
    half = QK_ROPE // 2
    return jnp.concatenate([w[..., half:], w[..., :half]], axis=-1)


def kernel(x_prompt, x_sample, state_conv, cache_ckv, cache_krope, page_table, ln_g, ln_b, conv_w_pw1, conv_b_pw1, conv_w_dw, conv_b_dw, conv_ln_g, conv_ln_b, conv_w_pw2, conv_b_pw2, mla_w_dq, mla_q_norm, mla_w_uq, mla_w_o, kv_w_dkv, kv_norm, kv_w_kr, kv_w_uk, kv_w_uv, ffn_w_gate, ffn_w_up, ffn_w_down):
    nb, s, d = x_prompt.shape
    bd, t, _ = x_sample.shape
    mp, ms = nb * s, bd * t
    depth = ln_g.shape[0]
    assert depth == 2 and conv_w_pw1.shape[0] == 1 and mla_w_dq.shape[0] == 1
    assert mp % ms == 0
    heads = mla_w_uq.shape[2]
    lora = kv_w_dkv.shape[1]
    ps = cache_ckv.shape[1]
    past = page_table.shape[1] * ps
    alpha = (2.0 * depth) ** 0.25
    scale = float(QK_NOPE + QK_ROPE) ** -0.5

    x = jnp.concatenate([x_prompt.reshape(mp, d), x_sample.reshape(ms, d)], axis=0)
    xb = x.astype(BF16)

    def vec(v):
        return v.reshape(1, -1)

    g = _pw1_glu(xb, conv_w_pw1[0].astype(BF16), vec(conv_b_pw1[0]))
    g_p, g_s = g[:mp].reshape(nb, s, d), g[mp:].reshape(bd, t, d)
    conv_args = (conv_w_dw[0], vec(conv_b_dw[0]), vec(conv_ln_g[0]), vec(conv_ln_b[0]))
    z_p = _conv_prompt(g_p, *conv_args)
    z_s, new_state = _conv_sample(state_conv[0], g_s, *conv_args)
    z = jnp.concatenate([z_p.reshape(mp, d), z_s.reshape(ms, d)], axis=0)
    conv_prompt = g_p[:, s - (conv_w_dw.shape[1] - 1):][None]
    conv_sample = new_state[None]

    def ffn(h, hb, l):
        mid = _gate_up(hb, ffn_w_gate[l].astype(BF16), ffn_w_up[l].astype(BF16))
        pre = _mm_residual(mid, ffn_w_down[l].astype(BF16), h, None, alpha, ffn_w_down.shape[1] // 2)
        return _ln(pre, vec(ln_g[l, 1]), vec(ln_b[l, 1]))

    pre = _mm_residual(z, conv_w_pw2[0].astype(BF16), x, vec(conv_b_pw2[0]), alpha, d)
    h1, h1b = _ln(pre, vec(ln_g[0, 0]), vec(ln_b[0, 0]))
    h2, h2b = ffn(h1, h1b, 0)

    cs_p = _rope_table(jnp.arange(s))
    cs_s = _rope_table(past + jnp.arange(t))
    cs = jnp.concatenate([jnp.tile(cs_p, (nb, 1)), jnp.tile(cs_s, (bd, 1))], axis=0)
    w_kv = jnp.concatenate([kv_w_dkv, kv_w_kr, _swap_halves(kv_w_kr)], axis=1).astype(BF16)
    ckv, krope = _shared_kv(h2b, w_kv, vec(kv_norm), cs)
    cq = _dq(h2b, mla_w_dq[0].astype(BF16), vec(mla_q_norm[0]))
    w_uq = mla_w_uq[0]
    w_uq_ext = jnp.concatenate([w_uq, _swap_halves(w_uq[..., QK_NOPE:])], axis=-1)
    q_ext = _uq(cq, w_uq_ext.reshape(w_uq.shape[0], heads * HEAD_EXT).astype(BF16), cs, scale)

    wuk2d = kv_w_uk.reshape(lora, heads * QK_NOPE).astype(BF16)
    wuv2d = kv_w_uv.reshape(lora, heads * V_DIM).astype(BF16)
    wuvt = jnp.transpose(kv_w_uv, (1, 2, 0)).reshape(heads * V_DIM, lora).astype(BF16)
    ckv_b = ckv.astype(BF16)
    krz = jnp.pad(krope, ((0, 0), (0, LANES - QK_ROPE))).astype(BF16)

    o_p = _prompt_attn(q_ext, ckv_b, krz, wuk2d, wuvt, nb, s)

    ql, qr = _q_latent(q_ext, wuk2d, mp // ms, ms)
    rows = t * heads
    nk = LANES
    cn_pad = jnp.pad(ckv_b[mp:].reshape(bd, t, lora), ((0, 0), (0, nk - t), (0, 0)))
    rn_pad = jnp.pad(krz[mp:].reshape(bd, t, LANES), ((0, 0), (0, nk - t), (0, 0)))
    o_lat = _sample_attn(page_table, ql.reshape(bd, rows, lora), qr.reshape(bd, rows, LANES),
                         cn_pad, rn_pad, cache_ckv, cache_krope, heads)
    o_s = _latent_to_v(o_lat.reshape(ms, heads * lora), wuv2d)

    o = jnp.concatenate([o_p, o_s], axis=0)
    w_o = mla_w_o[0].reshape(heads * V_DIM, d).astype(BF16)
    pre = _mm_residual(o, w_o, h2, None, alpha, 4096)
    h3, h3b = _ln(pre, vec(ln_g[1, 0]), vec(ln_b[1, 0]))
    y, _ = ffn(h3, h3b, 1)

    return (y[:mp].reshape(nb, s, d), y[mp:].reshape(bd, t, d), conv_prompt, conv_sample,
            ckv[:mp].reshape(nb, s, lora), ckv[mp:].reshape(bd, t, lora),
            krope[:mp].reshape(nb, s, QK_ROPE), krope[mp:].reshape(bd, t, QK_ROPE))
```

```python
import functools
import math

import jax
import jax.numpy as jnp
from jax import lax
from jax.experimental import pallas as pl
from jax.experimental.pallas import tpu as pltpu

F32 = jnp.float32
BF16 = jnp.bfloat16

LN_EPS = 1e-5
RMS_EPS = 1e-6
ROPE_THETA = 10000.0
QK_NOPE = 128
QK_ROPE = 64
V_DIM = 128
HEAD_EXT = 256
LANES = 128
SUBLANES = 8
V7X_VMEM_LIMIT = 56 * 1024 * 1024

_NT = (((1,), (1,)), ((), ()))


def _params(*sem):
    return pltpu.CompilerParams(dimension_semantics=sem, vmem_limit_bytes=V7X_VMEM_LIMIT)


def _blk(dim, pref, unit=LANES):
    if dim <= pref:
        return dim
    b = (pref // unit) * unit
    while b > unit and dim % b:
        b -= unit
    assert dim % b == 0, (dim, pref, unit)
    return b


def _layer_norm(x, g, b):
    mu = jnp.mean(x, axis=-1, keepdims=True)
    xc = x - mu
    var = jnp.mean(xc * xc, axis=-1, keepdims=True)
    return xc * lax.rsqrt(var + LN_EPS) * g + b


def _glu_kernel(x_ref, wa_ref, wb_ref, ba_ref, bb_ref, o_ref):
    x = x_ref[...].astype(BF16)
    a = jnp.dot(x, wa_ref[...], preferred_element_type=F32) + ba_ref[...]
    b = jnp.dot(x, wb_ref[...], preferred_element_type=F32) + bb_ref[...]
    o_ref[...] = a * jax.nn.sigmoid(b)


def _pw1_glu(x, w, bias):
    m, k = x.shape
    d = w.shape[1] // 2
    bm, bn = _blk(m, 512, 8), _blk(d, 512)
    nj = d // bn
    return pl.pallas_call(
        _glu_kernel,
        grid=(m // bm, nj),
        in_specs=[
            pl.BlockSpec((bm, k), lambda i, j: (i, 0)),
            pl.BlockSpec((k, bn), lambda i, j: (0, j)),
            pl.BlockSpec((k, bn), lambda i, j: (0, j + nj)),
            pl.BlockSpec((1, bn), lambda i, j: (0, j)),
            pl.BlockSpec((1, bn), lambda i, j: (0, j + nj)),
        ],
        out_specs=pl.BlockSpec((bm, bn), lambda i, j: (i, j)),
        out_shape=jax.ShapeDtypeStruct((m, d), F32),
        compiler_params=_params("parallel", "arbitrary"),
        name="pw1_glu",
    )(x, w, w, bias, bias)


def _conv_prompt_kernel(g_ref, halo_ref, w_ref, bdw_ref, lg_ref, lb_ref, z_ref, win_ref, sh_ref, y_ref,
                        *, tt, kw, halo, rt, cw):
    i = pl.program_id(1)
    d = y_ref.shape[1]
    keep = jnp.where(i == 0, 0.0, 1.0).astype(F32)
    win_ref[0:halo, :] = halo_ref[0] * keep
    win_ref[halo:halo + tt, :] = g_ref[0]
    base = halo - (kw - 1)
    nrow = sh_ref.shape[1]

    def chunk(c, carry):
        off = pl.multiple_of(c * cw, cw)
        for r in range(1, SUBLANES):
            sh_ref[r - 1] = win_ref[pl.ds(r, nrow), pl.ds(off, cw)]
        for rb in range(tt // rt):
            acc = jnp.zeros((rt, cw), F32)
            for k in range(kw):
                s = k + base
                r = s % SUBLANES
                a = rb * rt + s - r
                if r == 0:
                    x = win_ref[pl.ds(a, rt), pl.ds(off, cw)]
                else:
                    x = sh_ref[r - 1, pl.ds(a, rt), :]
                acc = acc + w_ref[k:k + 1, pl.ds(off, cw)] * x
            y_ref[rb * rt:(rb + 1) * rt, pl.ds(off, cw)] = acc + bdw_ref[:, pl.ds(off, cw)]
        return carry

    lax.fori_loop(0, d // cw, chunk, 0)
    y = _layer_norm(y_ref[...], lg_ref[...], lb_ref[...])
    z_ref[0] = (y * jax.nn.sigmoid(y)).astype(BF16)


def _conv_prompt(g3, w_dw, b_dw, ln_g, ln_b):
    b, s, d = g3.shape
    kw = w_dw.shape[0]
    halo = 32
    assert kw - 1 <= halo
    tt = _blk(s, 256, halo)
    rt, cw = _blk(tt, 128, SUBLANES), LANES
    hb = tt // halo
    kern = functools.partial(_conv_prompt_kernel, tt=tt, kw=kw, halo=halo, rt=rt, cw=cw)
    return pl.pallas_call(
        kern,
        grid=(b, s // tt),
        in_specs=[
            pl.BlockSpec((1, tt, d), lambda bi, i: (bi, i, 0)),
            pl.BlockSpec((1, halo, d), lambda bi, i: (bi, jnp.maximum(i * hb - 1, 0), 0)),
            pl.BlockSpec((kw, d), lambda bi, i: (0, 0)),
            pl.BlockSpec((1, d), lambda bi, i: (0, 0)),
            pl.BlockSpec((1, d), lambda bi, i: (0, 0)),
            pl.BlockSpec((1, d), lambda bi, i: (0, 0)),
        ],
        out_specs=pl.BlockSpec((1, tt, d), lambda bi, i: (bi, i, 0)),
        out_shape=jax.ShapeDtypeStruct((b, s, d), BF16),
        scratch_shapes=[pltpu.VMEM((halo + tt, d), F32),
                        pltpu.VMEM((SUBLANES - 1, halo + tt - SUBLANES, cw), F32),
                        pltpu.VMEM((tt, d), F32)],
        compiler_params=_params("parallel", "arbitrary"),
        name="conv_prompt",
    )(g3, g3, w_dw, b_dw, ln_g, ln_b)


def _conv_sample_kernel(st_ref, g_ref, w_ref, bdw_ref, lg_ref, lb_ref, z_ref, ns_ref, pad_ref, *, t, kw):
    cs = kw - 1
    pad_ref[:, 0:cs, :] = st_ref[...]
    pad_ref[:, cs:cs + t, :] = g_ref[...]
    acc = jnp.zeros(g_ref.shape, F32)
    for k in range(kw):
        acc = acc + w_ref[k:k + 1, :][None] * pad_ref[:, k:k + t, :]
    y = _layer_norm(acc + bdw_ref[...][None], lg_ref[...][None], lb_ref[...][None])
    z_ref[...] = (y * jax.nn.sigmoid(y)).astype(BF16)
    ns_ref[...] = pad_ref[:, t:t + cs, :]


def _conv_sample(state, g3, w_dw, b_dw, ln_g, ln_b):
    bd, t, d = g3.shape
    kw = w_dw.shape[0]
    cs = kw - 1
    bb = _blk(bd, 8, 1)
    kern = functools.partial(_conv_sample_kernel, t=t, kw=kw)
    return pl.pallas_call(
        kern,
        grid=(bd // bb,),
        in_specs=[
            pl.BlockSpec((bb, cs, d), lambda i: (i, 0, 0)),
            pl.BlockSpec((bb, t, d), lambda i: (i, 0, 0)),
            pl.BlockSpec((kw, d), lambda i: (0, 0)),
            pl.BlockSpec((1, d), lambda i: (0, 0)),
            pl.BlockSpec((1, d), lambda i: (0, 0)),
            pl.BlockSpec((1, d), lambda i: (0, 0)),
        ],
        out_specs=[pl.BlockSpec((bb, t, d), lambda i: (i, 0, 0)),
                   pl.BlockSpec((bb, cs, d), lambda i: (i, 0, 0))],
        out_shape=[jax.ShapeDtypeStruct((bd, t, d), BF16), jax.ShapeDtypeStruct((bd, cs, d), F32)],
        scratch_shapes=[pltpu.VMEM((bb, cs + t, d), F32)],
        compiler_params=_params("parallel"),
        name="conv_sample",
    )(state, g3, w_dw, b_dw, ln_g, ln_b)


def _mm_res_kernel(*refs, alpha, has_bias):
    if has_bias:
        x_ref, w_ref, r_ref, b_ref, o_ref = refs
    else:
        x_ref, w_ref, r_ref, o_ref = refs
        b_ref = None
    k = pl.program_id(2)

    @pl.when(k == 0)
    def _():
        init = alpha * r_ref[...]
        if has_bias:
            init = init + b_ref[...]
        o_ref[...] = init

    o_ref[...] += jnp.dot(x_ref[...], w_ref[...], preferred_element_type=F32)


def _mm_residual(xb, w3, layer, resid, bias, alpha, tk_pref):
    m, k = xb.shape
    n = w3.shape[2]
    bm, bn, tk = _blk(m, 1024, 8), _blk(n, 512), _blk(k, tk_pref)
    has_bias = bias is not None
    in_specs = [
        pl.BlockSpec((bm, tk), lambda i, j, kk: (i, kk)),
        pl.BlockSpec((None, tk, bn), lambda i, j, kk: (layer, kk, j)),
        pl.BlockSpec((bm, bn), lambda i, j, kk: (i, j)),
    ]
    args = [xb, w3, resid]
    if has_bias:
        in_specs.append(pl.BlockSpec((1, bn), lambda i, j, kk: (0, j)))
        args.append(bias)
    return pl.pallas_call(
        functools.partial(_mm_res_kernel, alpha=alpha, has_bias=has_bias),
        grid=(m // bm, n // bn, k // tk),
        in_specs=in_specs,
        out_specs=pl.BlockSpec((bm, bn), lambda i, j, kk: (i, j)),
        out_shape=jax.ShapeDtypeStruct((m, n), F32),
        compiler_params=_params("parallel", "parallel", "arbitrary"),
        name="mm_residual",
    )(*args)


def _ln_kernel(x_ref, g_ref, b_ref, o_ref, ob_ref):
    y = _layer_norm(x_ref[...], g_ref[...], b_ref[...])
    o_ref[...] = y
    ob_ref[...] = y.astype(BF16)


def _ln(x, g, b):
    m, d = x.shape
    bm = _blk(m, 256, 8)
    return pl.pallas_call(
        _ln_kernel,
        grid=(m // bm,),
        in_specs=[pl.BlockSpec((bm, d), lambda i: (i, 0)),
                  pl.BlockSpec((1, d), lambda i: (0, 0)),
                  pl.BlockSpec((1, d), lambda i: (0, 0))],
        out_specs=[pl.BlockSpec((bm, d), lambda i: (i, 0)), pl.BlockSpec((bm, d), lambda i: (i, 0))],
        out_shape=[jax.ShapeDtypeStruct((m, d), F32), jax.ShapeDtypeStruct((m, d), BF16)],
        compiler_params=_params("parallel"),
        name="layer_norm",
    )(x, g, b)


def _gate_up_kernel(x_ref, wg_ref, wu_ref, o_ref):
    x = x_ref[...]
    a = jnp.dot(x, wg_ref[...], preferred_element_type=F32)
    b = jnp.dot(x, wu_ref[...], preferred_element_type=F32)
    o_ref[...] = (a * jax.nn.sigmoid(a) * b).astype(BF16)


def _gate_up(xb, wg3, wu3, layer):
    m, k = xb.shape
    n = wg3.shape[2]
    bm, bn = _blk(m, 1024, 8), _blk(n, 512)
    return pl.pallas_call(
        _gate_up_kernel,
        grid=(m // bm, n // bn),
        in_specs=[pl.BlockSpec((bm, k), lambda i, j: (i, 0)),
                  pl.BlockSpec((None, k, bn), lambda i, j: (layer, 0, j)),
                  pl.BlockSpec((None, k, bn), lambda i, j: (layer, 0, j))],
        out_specs=pl.BlockSpec((bm, bn), lambda i, j: (i, j)),
        out_shape=jax.ShapeDtypeStruct((m, n), BF16),
        compiler_params=_params("parallel", "arbitrary"),
        name="ffn_gate_up",
    )(xb, wg3, wu3)


def _dq_kernel(x_ref, w_ref, g_ref, o_ref):
    y = jnp.dot(x_ref[...], w_ref[...], preferred_element_type=F32)
    ms = jnp.mean(y * y, axis=-1, keepdims=True)
    o_ref[...] = (y * lax.rsqrt(ms + RMS_EPS) * g_ref[...]).astype(BF16)


def _dq(xb, w, g):
    m, k = xb.shape
    n = w.shape[1]
    bm = _blk(m, 512, 8)
    return pl.pallas_call(
        _dq_kernel,
        grid=(m // bm,),
        in_specs=[pl.BlockSpec((bm, k), lambda i: (i, 0)),
                  pl.BlockSpec((k, n), lambda i: (0, 0)),
                  pl.BlockSpec((1, n), lambda i: (0, 0))],
        out_specs=pl.BlockSpec((bm, n), lambda i: (i, 0)),
        out_shape=jax.ShapeDtypeStruct((m, n), BF16),
        compiler_params=_params("parallel"),
        name="mla_dq",
    )(xb, w, g)


def _kv_kernel(x_ref, w_ref, g_ref, cs_ref, c_ref, cb_ref, kr_ref, krz_ref, *, lora):
    y = jnp.dot(x_ref[...], w_ref[...], preferred_element_type=F32)
    c = y[:, :lora]
    ms = jnp.mean(c * c, axis=-1, keepdims=True)
    c = c * lax.rsqrt(ms + RMS_EPS) * g_ref[...]
    c_ref[...] = c
    cb_ref[...] = c.astype(BF16)
    t = y[:, lora:lora + 2 * QK_ROPE] * cs_ref[...]
    kr = t + pltpu.roll(t, QK_ROPE, 1)
    kr_ref[...] = kr[:, :QK_ROPE]
    lane = lax.broadcasted_iota(jnp.int32, kr.shape, 1)
    krz_ref[...] = jnp.where(lane < QK_ROPE, kr, 0.0).astype(BF16)


def _shared_kv(xb, w, g, cs):
    m, k = xb.shape
    n = w.shape[1]
    lora = n - 2 * QK_ROPE
    bm = _blk(m, 512, 8)
    return pl.pallas_call(
        functools.partial(_kv_kernel, lora=lora),
        grid=(m // bm,),
        in_specs=[pl.BlockSpec((bm, k), lambda i: (i, 0)),
                  pl.BlockSpec((k, n), lambda i: (0, 0)),
                  pl.BlockSpec((1, lora), lambda i: (0, 0)),
                  pl.BlockSpec((bm, 2 * QK_ROPE), lambda i: (i, 0))],
        out_specs=[pl.BlockSpec((bm, lora), lambda i: (i, 0)),
                   pl.BlockSpec((bm, lora), lambda i: (i, 0)),
                   pl.BlockSpec((bm, QK_ROPE), lambda i: (i, 0)),
                   pl.BlockSpec((bm, LANES), lambda i: (i, 0))],
        out_shape=[jax.ShapeDtypeStruct((m, lora), F32), jax.ShapeDtypeStruct((m, lora), BF16),
                   jax.ShapeDtypeStruct((m, QK_ROPE), F32), jax.ShapeDtypeStruct((m, LANES), BF16)],
        compiler_params=_params("parallel"),
        name="mla_shared_kv",
    )(xb, w, g, cs)


def _uq_kernel(x_ref, w_ref, cs_ref, o_ref, *, hb, scale):
    y = jnp.dot(x_ref[...], w_ref[...], preferred_element_type=F32)
    cs = cs_ref[...]
    for h in range(hb):
        lo = h * HEAD_EXT
        o_ref[:, lo:lo + QK_NOPE] = (y[:, lo:lo + QK_NOPE] * scale).astype(BF16)
        t = y[:, lo + QK_NOPE:lo + HEAD_EXT] * cs
        r = (t + pltpu.roll(t, QK_ROPE, 1)) * scale
        o_ref[:, lo + QK_NOPE:lo + HEAD_EXT] = r.astype(BF16)


def _uq(cq, w_ext, cs, scale):
    m, k = cq.shape
    n = w_ext.shape[1]
    bm, bn = _blk(m, 1024, 8), _blk(n, 1024, HEAD_EXT)
    return pl.pallas_call(
        functools.partial(_uq_kernel, hb=bn // HEAD_EXT, scale=scale),
        grid=(m // bm, n // bn),
        in_specs=[pl.BlockSpec((bm, k), lambda i, j: (i, 0)),
                  pl.BlockSpec((k, bn), lambda i, j: (0, j)),
                  pl.BlockSpec((bm, 2 * QK_ROPE), lambda i, j: (i, 0))],
        out_specs=pl.BlockSpec((bm, bn), lambda i, j: (i, j)),
        out_shape=jax.ShapeDtypeStruct((m, n), BF16),
        compiler_params=_params("parallel", "arbitrary"),
        name="mla_uq",
    )(cq, w_ext, cs)


def _prompt_attn_kernel(q_ref, c_ref, krz_ref, wuk_ref, wuvt_ref, o_ref, kext_ref, vt_ref, *, s, tq, hp):
    c = c_ref[...]
    kn = jnp.dot(c, wuk_ref[...], preferred_element_type=F32).astype(BF16)
    krz = krz_ref[...]
    for h in range(hp):
        kext_ref[h, :, :QK_NOPE] = kn[:, h * QK_NOPE:(h + 1) * QK_NOPE]
        kext_ref[h, :, QK_NOPE:] = krz
    vt_ref[...] = lax.dot_general(wuvt_ref[...], c, _NT, preferred_element_type=F32).astype(BF16)
    for h in range(hp):
        for qi in range(s // tq):
            q = q_ref[qi * tq:(qi + 1) * tq, h * HEAD_EXT:(h + 1) * HEAD_EXT]
            m = jnp.full((1, tq), -jnp.inf, F32)
            l = jnp.zeros((1, tq), F32)
            acc = jnp.zeros((V_DIM, tq), F32)
            for kj in range(qi + 1):
                k = kext_ref[h, kj * tq:(kj + 1) * tq, :]
                st = lax.dot_general(k, q, _NT, preferred_element_type=F32)
                if kj == qi:
                    key = lax.broadcasted_iota(jnp.int32, (tq, tq), 0)
                    qry = lax.broadcasted_iota(jnp.int32, (tq, tq), 1)
                    st = jnp.where(key <= qry, st, -jnp.inf)
                m_new = jnp.maximum(m, jnp.max(st, axis=0, keepdims=True))
                p = jnp.exp2(st - m_new)
                a = jnp.exp2(m - m_new)
                l = a * l + jnp.sum(p, axis=0, keepdims=True)
                acc = a * acc + jnp.dot(vt_ref[h * V_DIM:(h + 1) * V_DIM, kj * tq:(kj + 1) * tq],
                                        p.astype(BF16), preferred_element_type=F32)
                m = m_new
            o_ref[qi * tq:(qi + 1) * tq, h * V_DIM:(h + 1) * V_DIM] = (acc / l).T.astype(BF16)


def _prompt_attn(q_ext, cb, krz, wuk2d, wuvt, nb, s):
    h = wuk2d.shape[1] // QK_NOPE
    lora = cb.shape[1]
    tq = _blk(s, 1024)
    hp = 2 if h % 2 == 0 else 1
    return pl.pallas_call(
        functools.partial(_prompt_attn_kernel, s=s, tq=tq, hp=hp),
        grid=(nb, h // hp),
        in_specs=[pl.BlockSpec((s, hp * HEAD_EXT), lambda b, hh: (b, hh)),
                  pl.BlockSpec((s, lora), lambda b, hh: (b, 0)),
                  pl.BlockSpec((s, LANES), lambda b, hh: (b, 0)),
                  pl.BlockSpec((lora, hp * QK_NOPE), lambda b, hh: (0, hh)),
                  pl.BlockSpec((hp * V_DIM, lora), lambda b, hh: (hh, 0))],
        out_specs=pl.BlockSpec((s, hp * V_DIM), lambda b, hh: (b, hh)),
        out_shape=jax.ShapeDtypeStruct((nb * s, h * V_DIM), BF16),
        scratch_shapes=[pltpu.VMEM((hp, s, HEAD_EXT), BF16), pltpu.VMEM((hp * V_DIM, s), BF16)],
        compiler_params=_params("parallel", "arbitrary"),
        name="mla_prompt_attn",
    )(q_ext, cb, krz, wuk2d, wuvt)


def _qlat_kernel(q_ref, wuk_ref, ql_ref, qr_ref, *, heads, tok):
    bb = ql_ref.shape[0]
    for hp in range(heads // 2):
        ql, qr = [], []
        for h in (2 * hp, 2 * hp + 1):
            qn = q_ref[:, h * HEAD_EXT:h * HEAD_EXT + QK_NOPE]
            y = lax.dot_general(qn, wuk_ref[:, h * QK_NOPE:(h + 1) * QK_NOPE], _NT,
                                preferred_element_type=F32)
            ql.append(y.reshape(bb, tok, y.shape[1]))
            r = q_ref[:, h * HEAD_EXT + QK_NOPE:(h + 1) * HEAD_EXT].astype(F32)
            qr.append(r.reshape(bb, tok, LANES))
        lo = 2 * hp * tok
        ql_ref[:, lo:lo + 2 * tok, :] = jnp.concatenate(ql, axis=1).astype(BF16)
        qr_ref[:, lo:lo + 2 * tok, :] = jnp.concatenate(qr, axis=1).astype(BF16)


def _q_latent(q_ext, wuk2d, bd, tok):
    lora = wuk2d.shape[0]
    heads = wuk2d.shape[1] // QK_NOPE
    assert heads % 2 == 0 and tok == SUBLANES
    bb = _blk(bd, 16, 1)
    return pl.pallas_call(
        functools.partial(_qlat_kernel, heads=heads, tok=tok),
        grid=(bd // bb,),
        in_specs=[pl.BlockSpec((bb * tok, heads * HEAD_EXT), lambda i: (i, 0)),
                  pl.BlockSpec((lora, heads * QK_NOPE), lambda i: (0, 0))],
        out_specs=[pl.BlockSpec((bb, heads * tok, lora), lambda i: (i, 0, 0)),
                   pl.BlockSpec((bb, heads * tok, LANES), lambda i: (i, 0, 0))],
        out_shape=[jax.ShapeDtypeStruct((bd, heads * tok, lora), BF16),
                   jax.ShapeDtypeStruct((bd, heads * tok, LANES), BF16)],
        compiler_params=_params("parallel"),
        name="mla_q_latent",
    )(q_ext, wuk2d)


def _sample_attn_kernel(pt_ref, ql_ref, qr_ref, cn_ref, rnt_ref, *rest, pg, nsteps, ps, tok, tks):
    del pt_ref
    page_c, page_r = rest[:pg], rest[pg:2 * pg]
    o_ref, kc_ref, krt_ref, s_ref, m_ref, l_ref, acc_ref = rest[2 * pg:]
    step = pl.program_id(1)
    ql = ql_ref[0]
    qr = qr_ref[0]
    rows = ql.shape[0]

    @pl.when(step == 0)
    def _():
        cn = cn_ref[0]
        s = (lax.dot_general(ql, cn, _NT, preferred_element_type=F32)
             + jnp.dot(qr, rnt_ref[0], preferred_element_type=F32))
        nk = s.shape[1]
        row = lax.broadcasted_iota(jnp.int32, (rows, nk), 0)
        col = lax.broadcasted_iota(jnp.int32, (rows, nk), 1)
        s = jnp.where(col <= (row & (tok - 1)), s, -jnp.inf)
        m = jnp.max(s, axis=1, keepdims=True)
        p = jnp.exp2(s - m)
        m_ref[...] = jnp.broadcast_to(m, m_ref.shape)
        l_ref[...] = jnp.broadcast_to(jnp.sum(p, axis=1, keepdims=True), l_ref.shape)
        acc_ref[...] = jnp.dot(p.astype(BF16), cn, preferred_element_type=F32)
        krt_ref[QK_ROPE:, :] = jnp.zeros((LANES - QK_ROPE, krt_ref.shape[1]), BF16)

    nch = pg * ps // tks
    ppc = tks // ps

    def scores(c):
        for i in range(c * ppc, (c + 1) * ppc):
            kc_ref[i * ps:(i + 1) * ps, :] = page_c[i][0].astype(BF16)
            krt_ref[:QK_ROPE, i * ps:(i + 1) * ps] = page_r[i][0].astype(BF16)
        kc = kc_ref[c * tks:(c + 1) * tks, :]
        s_ref[c % 2] = (lax.dot_general(ql, kc, _NT, preferred_element_type=F32)
                        + jnp.dot(qr, krt_ref[:, c * tks:(c + 1) * tks], preferred_element_type=F32))

    def softmax_pv(c):
        s = s_ref[c % 2]
        m_prev = m_ref[...]
        m_new = jnp.maximum(m_prev, jnp.max(s, axis=1, keepdims=True))
        a = jnp.exp2(m_prev - m_new)
        p = jnp.exp2(s - m_new[:, :1])
        l_ref[...] = a * l_ref[...] + jnp.sum(p, axis=1, keepdims=True)
        acc_ref[...] = acc_ref[...] * a[:, :1] + jnp.dot(p.astype(BF16), kc_ref[c * tks:(c + 1) * tks, :],
                                                          preferred_element_type=F32)
        m_ref[...] = m_new

    scores(0)
    for c in range(nch):
        if c + 1 < nch:
            scores(c + 1)
        softmax_pv(c)

    @pl.when(step == nsteps - 1)
    def _():
        o_ref[0] = (acc_ref[...] / l_ref[...][:, :1]).astype(BF16)


def _sample_attn(page_table, ql3, qr3, cn_pad, rnt_pad, cache_ckv, cache_krt, tok):
    bd, rows, lora = ql3.shape
    npages = page_table.shape[1]
    ps = cache_ckv.shape[1]
    pg = _blk(npages, 32, 1)
    nsteps = npages // pg
    tks = _blk(pg * ps, 2048)
    nk = cn_pad.shape[1]
    assert tok & (tok - 1) == 0
    pt_flat = page_table.reshape(-1)

    def page_spec(i, shape):
        return pl.BlockSpec((1,) + shape, lambda b, st, pt: (pt[b * npages + st * pg + i], 0, 0))

    in_specs = [pl.BlockSpec((1, rows, lora), lambda b, st, pt: (b, 0, 0)),
                pl.BlockSpec((1, rows, LANES), lambda b, st, pt: (b, 0, 0)),
                pl.BlockSpec((1, nk, lora), lambda b, st, pt: (b, 0, 0)),
                pl.BlockSpec((1, LANES, nk), lambda b, st, pt: (b, 0, 0))]
    in_specs += [page_spec(i, (ps, lora)) for i in range(pg)]
    in_specs += [page_spec(i, (QK_ROPE, ps)) for i in range(pg)]
    kern = functools.partial(_sample_attn_kernel, pg=pg, nsteps=nsteps, ps=ps, tok=tok, tks=tks)
    return pl.pallas_call(
        kern,
        grid_spec=pltpu.PrefetchScalarGridSpec(
            num_scalar_prefetch=1,
            grid=(bd, nsteps),
            in_specs=in_specs,
            out_specs=pl.BlockSpec((1, rows, lora), lambda b, st, pt: (b, 0, 0)),
            scratch_shapes=[pltpu.VMEM((pg * ps, lora), BF16), pltpu.VMEM((LANES, pg * ps), BF16),
                            pltpu.VMEM((2, rows, tks), F32),
                            pltpu.VMEM((rows, LANES), F32), pltpu.VMEM((rows, LANES), F32),
                            pltpu.VMEM((rows, lora), F32)]),
        out_shape=jax.ShapeDtypeStruct((bd, rows, lora), BF16),
        compiler_params=_params("parallel", "arbitrary"),
        name="mla_sample_attn",
    )(pt_flat, ql3, qr3, cn_pad, rnt_pad, *([cache_ckv] * pg), *([cache_krt] * pg))


def _uv_kernel(o_ref, w_ref, y_ref, *, heads, tok):
    bb = o_ref.shape[0]
    for hp in range(heads // 2):
        x = o_ref[:, 2 * hp * tok:2 * (hp + 1) * tok, :]
        y = jnp.dot(x.reshape(bb * 2 * tok, x.shape[2]), w_ref[:, 2 * hp * V_DIM:2 * (hp + 1) * V_DIM],
                    preferred_element_type=F32).reshape(bb, 2 * tok, 2 * V_DIM)
        lo = 2 * hp * V_DIM
        y_ref[:, lo:lo + V_DIM] = y[:, :tok, :V_DIM].reshape(bb * tok, V_DIM).astype(BF16)
        y_ref[:, lo + V_DIM:lo + 2 * V_DIM] = y[:, tok:, V_DIM:].reshape(bb * tok, V_DIM).astype(BF16)


def _latent_to_v(o_lat, wuv2d, tok):
    bd, rows, lora = o_lat.shape
    heads = rows // tok
    assert heads % 2 == 0 and tok == SUBLANES
    bb = _blk(bd, 16, 1)
    return pl.pallas_call(
        functools.partial(_uv_kernel, heads=heads, tok=tok),
        grid=(bd // bb,),
        in_specs=[pl.BlockSpec((bb, rows, lora), lambda i: (i, 0, 0)),
                  pl.BlockSpec((lora, heads * V_DIM), lambda i: (0, 0))],
        out_specs=pl.BlockSpec((bb * tok, heads * V_DIM), lambda i: (i, 0)),
        out_shape=jax.ShapeDtypeStruct((bd * tok, heads * V_DIM), BF16),
        compiler_params=_params("parallel"),
        name="mla_latent_to_v",
    )(o_lat, wuv2d)


def _rope_table(pos):
    half = QK_ROPE // 2
    inv = 1.0 / (ROPE_THETA ** (jnp.arange(half, dtype=F32) / half))
    ang = pos.astype(F32)[:, None] * inv[None, :]
    cos, sin = jnp.cos(ang), jnp.sin(ang)
    return jnp.concatenate([cos, cos, -sin, sin], axis=1)


def _swap_halves(w):
    half = QK_ROPE // 2
    return jnp.concatenate([w[..., half:], w[..., :half]], axis=-1)


def kernel(x_prompt, x_sample, state_conv, cache_ckv, cache_krope, page_table, ln_g, ln_b, conv_w_pw1, conv_b_pw1, conv_w_dw, conv_b_dw, conv_ln_g, conv_ln_b, conv_w_pw2, conv_b_pw2, mla_w_dq, mla_q_norm, mla_w_uq, mla_w_o, kv_w_dkv, kv_norm, kv_w_kr, kv_w_uk, kv_w_uv, ffn_w_gate, ffn_w_up, ffn_w_down):
    nb, s, d = x_prompt.shape
    bd, t, _ = x_sample.shape
    depth = ln_g.shape[0]
    assert depth == 2 and conv_w_pw1.shape[0] == 1 and mla_w_dq.shape[0] == 1
    heads = mla_w_uq.shape[2]
    lora = kv_w_dkv.shape[1]
    ps = cache_ckv.shape[1]
    past = page_table.shape[1] * ps
    kw = conv_w_dw.shape[1]
    alpha = (2.0 * depth) ** 0.25
    q_scale = float(QK_NOPE + QK_ROPE) ** -0.5 * math.log2(math.e)

    def vec(v):
        return v.reshape(1, -1)

    w_pw1 = conv_w_pw1[0].astype(BF16)
    w_pw2 = conv_w_pw2.astype(BF16)
    wg3, wu3, wd3 = ffn_w_gate.astype(BF16), ffn_w_up.astype(BF16), ffn_w_down.astype(BF16)
    w_kv = jnp.concatenate([kv_w_dkv, kv_w_kr, _swap_halves(kv_w_kr)], axis=1).astype(BF16)
    w_dq = mla_w_dq[0].astype(BF16)
    w_uq = mla_w_uq[0]
    w_uq_ext = jnp.concatenate([w_uq, _swap_halves(w_uq[..., QK_NOPE:])], axis=-1)
    w_uq_ext = w_uq_ext.reshape(w_uq.shape[0], heads * HEAD_EXT).astype(BF16)
    wuk2d = kv_w_uk.reshape(lora, heads * QK_NOPE).astype(BF16)
    wuv2d = kv_w_uv.reshape(lora, heads * V_DIM).astype(BF16)
    wuvt = jnp.transpose(kv_w_uv, (1, 2, 0)).reshape(heads * V_DIM, lora).astype(BF16)
    w_o3 = mla_w_o.reshape(1, heads * V_DIM, d).astype(BF16)
    conv_args = (conv_w_dw[0], vec(conv_b_dw[0]), vec(conv_ln_g[0]), vec(conv_ln_b[0]))

    def ffn(h, hb, l):
        mid = _gate_up(hb, wg3, wu3, l)
        pre = _mm_residual(mid, wd3, l, h, None, alpha, ffn_w_down.shape[1] // 2)
        return _ln(pre, vec(ln_g[l, 1]), vec(ln_b[l, 1]))

    def layer0(x, conv):
        g = _pw1_glu(x, w_pw1, vec(conv_b_pw1[0]))
        z, extra = conv(g)
        pre = _mm_residual(z, w_pw2, 0, x, vec(conv_b_pw2[0]), alpha, d)
        h1, h1b = _ln(pre, vec(ln_g[0, 0]), vec(ln_b[0, 0]))
        h2, h2b = ffn(h1, h1b, 0)
        return h2, h2b, extra

    def qkv(h2b, cs):
        c, cb, kr, krz = _shared_kv(h2b, w_kv, vec(kv_norm), cs)
        cq = _dq(h2b, w_dq, vec(mla_q_norm[0]))
        return c, cb, kr, krz, _uq(cq, w_uq_ext, cs, q_scale)

    def layer1_tail(o, h2):
        pre = _mm_residual(o, w_o3, 0, h2, None, alpha, 4096)
        h3, h3b = _ln(pre, vec(ln_g[1, 0]), vec(ln_b[1, 0]))
        return ffn(h3, h3b, 1)[0]

    def conv_p(g):
        g3 = g.reshape(nb, s, d)
        return _conv_prompt(g3, *conv_args).reshape(nb * s, d), g3[:, s - (kw - 1):]

    h2, h2b, conv_prompt = layer0(x_prompt.reshape(nb * s, d), conv_p)
    cs_p = jnp.tile(_rope_table(jnp.arange(s)), (nb, 1))
    ckv_p, cb_p, kr_p, krz_p, q_p = qkv(h2b, cs_p)
    o_p = _prompt_attn(q_p, cb_p, krz_p, wuk2d, wuvt, nb, s)
    y_p = layer1_tail(o_p, h2)

    def conv_s(g):
        z, new_state = _conv_sample(state_conv[0], g.reshape(bd, t, d), *conv_args)
        return z.reshape(bd * t, d), new_state

    g2, g2b, conv_sample = layer0(x_sample.reshape(bd * t, d), conv_s)
    cs_s = jnp.tile(_rope_table(past + jnp.arange(t)), (bd, 1))
    ckv_s, cb_s, kr_s, krz_s, q_s = qkv(g2b, cs_s)
    ql3, qr3 = _q_latent(q_s, wuk2d, bd, t)
    nk = LANES
    cn_pad = jnp.pad(cb_s.reshape(bd, t, lora), ((0, 0), (0, nk - t), (0, 0)))
    rnt_pad = jnp.pad(jnp.swapaxes(krz_s.reshape(bd, t, LANES), 1, 2), ((0, 0), (0, 0), (0, nk - t)))
    cache_krt = jnp.swapaxes(cache_krope, 1, 2)
    o_lat = _sample_attn(page_table, ql3, qr3, cn_pad, rnt_pad, cache_ckv, cache_krt, t)
    o_s = _latent_to_v(o_lat, wuv2d, t)
    y_s = layer1_tail(o_s, g2)

    return (y_p.reshape(nb, s, d), y_s.reshape(bd, t, d), conv_prompt[None], conv_sample[None],
            ckv_p.reshape(nb, s, lora), ckv_s.reshape(bd, t, lora),
            kr_p.reshape(nb, s, QK_ROPE), kr_s.reshape(bd, t, QK_ROPE))
```

```python
import functools
import math

import jax
import jax.numpy as jnp
from jax import lax
from jax.experimental import pallas as pl
from jax.experimental.pallas import tpu as pltpu

F32 = jnp.float32
BF16 = jnp.bfloat16

LN_EPS = 1e-5
RMS_EPS = 1e-6
ROPE_THETA = 10000.0
QK_NOPE = 128
QK_ROPE = 64
V_DIM = 128
HEAD_EXT = 256
LANES = 128
SUBLANES = 8
V7X_VMEM_LIMIT = 56 * 1024 * 1024

_NT = (((1,), (1,)), ((), ()))


def _params(*sem):
    return pltpu.CompilerParams(dimension_semantics=sem, vmem_limit_bytes=V7X_VMEM_LIMIT)


def _blk(dim, pref, unit=LANES):
    if dim <= pref:
        return dim
    b = (pref // unit) * unit
    while b > unit and dim % b:
        b -= unit
    assert dim % b == 0, (dim, pref, unit)
    return b


def _layer_norm(x, g, b):
    mu = jnp.mean(x, axis=-1, keepdims=True)
    xc = x - mu
    var = jnp.mean(xc * xc, axis=-1, keepdims=True)
    return xc * lax.rsqrt(var + LN_EPS) * g + b


def _glu_kernel(x_ref, wa_ref, wb_ref, ba_ref, bb_ref, o_ref):
    x = x_ref[...].astype(BF16)
    a = jnp.dot(x, wa_ref[...], preferred_element_type=F32) + ba_ref[...]
    b = jnp.dot(x, wb_ref[...], preferred_element_type=F32) + bb_ref[...]
    o_ref[...] = a * jax.nn.sigmoid(b)


def _pw1_glu(x, w, bias):
    m, k = x.shape
    d = w.shape[1] // 2
    bm, bn = _blk(m, 512, 8), _blk(d, 512)
    nj = d // bn
    return pl.pallas_call(
        _glu_kernel,
        grid=(m // bm, nj),
        in_specs=[
            pl.BlockSpec((bm, k), lambda i, j: (i, 0)),
            pl.BlockSpec((k, bn), lambda i, j: (0, j)),
            pl.BlockSpec((k, bn), lambda i, j: (0, j + nj)),
            pl.BlockSpec((1, bn), lambda i, j: (0, j)),
            pl.BlockSpec((1, bn), lambda i, j: (0, j + nj)),
        ],
        out_specs=pl.BlockSpec((bm, bn), lambda i, j: (i, j)),
        out_shape=jax.ShapeDtypeStruct((m, d), F32),
        compiler_params=_params("parallel", "arbitrary"),
        name="pw1_glu",
    )(x, w, w, bias, bias)


def _conv_prompt_kernel(g_ref, halo_ref, w_ref, bdw_ref, lg_ref, lb_ref, z_ref, win_ref, sh_ref, y_ref,
                        *, tt, kw, halo, rt, cw):
    i = pl.program_id(1)
    d = y_ref.shape[1]
    keep = jnp.where(i == 0, 0.0, 1.0).astype(F32)
    win_ref[0:halo, :] = halo_ref[0] * keep
    win_ref[halo:halo + tt, :] = g_ref[0]
    base = halo - (kw - 1)
    nrow = sh_ref.shape[1]

    def chunk(c, carry):
        off = pl.multiple_of(c * cw, cw)
        for r in range(1, SUBLANES):
            sh_ref[r - 1] = win_ref[pl.ds(r, nrow), pl.ds(off, cw)]
        for rb in range(tt // rt):
            acc = jnp.zeros((rt, cw), F32)
            for k in range(kw):
                s = k + base
                r = s % SUBLANES
                a = rb * rt + s - r
                if r == 0:
                    x = win_ref[pl.ds(a, rt), pl.ds(off, cw)]
                else:
                    x = sh_ref[r - 1, pl.ds(a, rt), :]
                acc = acc + w_ref[k:k + 1, pl.ds(off, cw)] * x
            y_ref[rb * rt:(rb + 1) * rt, pl.ds(off, cw)] = acc + bdw_ref[:, pl.ds(off, cw)]
        return carry

    lax.fori_loop(0, d // cw, chunk, 0)
    y = _layer_norm(y_ref[...], lg_ref[...], lb_ref[...])
    z_ref[0] = (y * jax.nn.sigmoid(y)).astype(BF16)


def _conv_prompt(g3, w_dw, b_dw, ln_g, ln_b):
    b, s, d = g3.shape
    kw = w_dw.shape[0]
    halo = 32
    assert kw - 1 <= halo
    tt = _blk(s, 256, halo)
    rt, cw = _blk(tt, 128, SUBLANES), LANES
    hb = tt // halo
    kern = functools.partial(_conv_prompt_kernel, tt=tt, kw=kw, halo=halo, rt=rt, cw=cw)
    return pl.pallas_call(
        kern,
        grid=(b, s // tt),
        in_specs=[
            pl.BlockSpec((1, tt, d), lambda bi, i: (bi, i, 0)),
            pl.BlockSpec((1, halo, d), lambda bi, i: (bi, jnp.maximum(i * hb - 1, 0), 0)),
            pl.BlockSpec((kw, d), lambda bi, i: (0, 0)),
            pl.BlockSpec((1, d), lambda bi, i: (0, 0)),
            pl.BlockSpec((1, d), lambda bi, i: (0, 0)),
            pl.BlockSpec((1, d), lambda bi, i: (0, 0)),
        ],
        out_specs=pl.BlockSpec((1, tt, d), lambda bi, i: (bi, i, 0)),
        out_shape=jax.ShapeDtypeStruct((b, s, d), BF16),
        scratch_shapes=[pltpu.VMEM((halo + tt, d), F32),
                        pltpu.VMEM((SUBLANES - 1, halo + tt - SUBLANES, cw), F32),
                        pltpu.VMEM((tt, d), F32)],
        compiler_params=_params("parallel", "arbitrary"),
        name="conv_prompt",
    )(g3, g3, w_dw, b_dw, ln_g, ln_b)


def _conv_sample_kernel(st_ref, g_ref, w_ref, bdw_ref, lg_ref, lb_ref, z_ref, ns_ref, pad_ref, *, t, kw):
    cs = kw - 1
    pad_ref[:, 0:cs, :] = st_ref[...]
    pad_ref[:, cs:cs + t, :] = g_ref[...]
    acc = jnp.zeros(g_ref.shape, F32)
    for k in range(kw):
        acc = acc + w_ref[k:k + 1, :][None] * pad_ref[:, k:k + t, :]
    y = _layer_norm(acc + bdw_ref[...][None], lg_ref[...][None], lb_ref[...][None])
    z_ref[...] = (y * jax.nn.sigmoid(y)).astype(BF16)
    ns_ref[...] = pad_ref[:, t:t + cs, :]


def _conv_sample(state, g3, w_dw, b_dw, ln_g, ln_b):
    bd, t, d = g3.shape
    kw = w_dw.shape[0]
    cs = kw - 1
    bb = _blk(bd, 8, 1)
    kern = functools.partial(_conv_sample_kernel, t=t, kw=kw)
    return pl.pallas_call(
        kern,
        grid=(bd // bb,),
        in_specs=[
            pl.BlockSpec((bb, cs, d), lambda i: (i, 0, 0)),
            pl.BlockSpec((bb, t, d), lambda i: (i, 0, 0)),
            pl.BlockSpec((kw, d), lambda i: (0, 0)),
            pl.BlockSpec((1, d), lambda i: (0, 0)),
            pl.BlockSpec((1, d), lambda i: (0, 0)),
            pl.BlockSpec((1, d), lambda i: (0, 0)),
        ],
        out_specs=[pl.BlockSpec((bb, t, d), lambda i: (i, 0, 0)),
                   pl.BlockSpec((bb, cs, d), lambda i: (i, 0, 0))],
        out_shape=[jax.ShapeDtypeStruct((bd, t, d), BF16), jax.ShapeDtypeStruct((bd, cs, d), F32)],
        scratch_shapes=[pltpu.VMEM((bb, cs + t, d), F32)],
        compiler_params=_params("parallel"),
        name="conv_sample",
    )(state, g3, w_dw, b_dw, ln_g, ln_b)


def _mm_res_kernel(*refs, alpha, has_bias):
    if has_bias:
        x_ref, w_ref, r_ref, b_ref, o_ref = refs
    else:
        x_ref, w_ref, r_ref, o_ref = refs
        b_ref = None
    k = pl.program_id(2)

    @pl.when(k == 0)
    def _():
        init = alpha * r_ref[...]
        if has_bias:
            init = init + b_ref[...]
        o_ref[...] = init

    o_ref[...] += jnp.dot(x_ref[...], w_ref[...], preferred_element_type=F32)


def _mm_residual(xb, w3, layer, resid, bias, alpha, tk_pref):
    m, k = xb.shape
    n = w3.shape[2]
    bm, bn, tk = _blk(m, 1024, 8), _blk(n, 512), _blk(k, tk_pref)
    has_bias = bias is not None
    in_specs = [
        pl.BlockSpec((bm, tk), lambda i, j, kk: (i, kk)),
        pl.BlockSpec((None, tk, bn), lambda i, j, kk: (layer, kk, j)),
        pl.BlockSpec((bm, bn), lambda i, j, kk: (i, j)),
    ]
    args = [xb, w3, resid]
    if has_bias:
        in_specs.append(pl.BlockSpec((1, bn), lambda i, j, kk: (0, j)))
        args.append(bias)
    return pl.pallas_call(
        functools.partial(_mm_res_kernel, alpha=alpha, has_bias=has_bias),
        grid=(m // bm, n // bn, k // tk),
        in_specs=in_specs,
        out_specs=pl.BlockSpec((bm, bn), lambda i, j, kk: (i, j)),
        out_shape=jax.ShapeDtypeStruct((m, n), F32),
        compiler_params=_params("parallel", "parallel", "arbitrary"),
        name="mm_residual",
    )(*args)


def _ln_kernel(x_ref, g_ref, b_ref, o_ref, ob_ref):
    y = _layer_norm(x_ref[...], g_ref[...], b_ref[...])
    o_ref[...] = y
    ob_ref[...] = y.astype(BF16)


def _ln(x, g, b):
    m, d = x.shape
    bm = _blk(m, 256, 8)
    return pl.pallas_call(
        _ln_kernel,
        grid=(m // bm,),
        in_specs=[pl.BlockSpec((bm, d), lambda i: (i, 0)),
                  pl.BlockSpec((1, d), lambda i: (0, 0)),
                  pl.BlockSpec((1, d), lambda i: (0, 0))],
        out_specs=[pl.BlockSpec((bm, d), lambda i: (i, 0)), pl.BlockSpec((bm, d), lambda i: (i, 0))],
        out_shape=[jax.ShapeDtypeStruct((m, d), F32), jax.ShapeDtypeStruct((m, d), BF16)],
        compiler_params=_params("parallel"),
        name="layer_norm",
    )(x, g, b)


def _gate_up_kernel(x_ref, wg_ref, wu_ref, o_ref):
    x = x_ref[...]
    a = jnp.dot(x, wg_ref[...].astype(BF16), preferred_element_type=F32)
    b = jnp.dot(x, wu_ref[...].astype(BF16), preferred_element_type=F32)
    o_ref[...] = (a * jax.nn.sigmoid(a) * b).astype(BF16)


def _gate_up(xb, wg3, wu3, layer):
    m, k = xb.shape
    n = wg3.shape[2]
    bm, bn = _blk(m, 2048, 8), _blk(n, 256)
    return pl.pallas_call(
        _gate_up_kernel,
        grid=(m // bm, n // bn),
        in_specs=[pl.BlockSpec((bm, k), lambda i, j: (i, 0)),
                  pl.BlockSpec((None, k, bn), lambda i, j: (layer, 0, j)),
                  pl.BlockSpec((None, k, bn), lambda i, j: (layer, 0, j))],
        out_specs=pl.BlockSpec((bm, bn), lambda i, j: (i, j)),
        out_shape=jax.ShapeDtypeStruct((m, n), BF16),
        compiler_params=_params("parallel", "arbitrary"),
        name="ffn_gate_up",
    )(xb, wg3, wu3)


def _dq_kernel(x_ref, w_ref, g_ref, o_ref):
    y = jnp.dot(x_ref[...], w_ref[...], preferred_element_type=F32)
    ms = jnp.mean(y * y, axis=-1, keepdims=True)
    o_ref[...] = (y * lax.rsqrt(ms + RMS_EPS) * g_ref[...]).astype(BF16)


def _dq(xb, w, g):
    m, k = xb.shape
    n = w.shape[1]
    bm = _blk(m, 512, 8)
    return pl.pallas_call(
        _dq_kernel,
        grid=(m // bm,),
        in_specs=[pl.BlockSpec((bm, k), lambda i: (i, 0)),
                  pl.BlockSpec((k, n), lambda i: (0, 0)),
                  pl.BlockSpec((1, n), lambda i: (0, 0))],
        out_specs=pl.BlockSpec((bm, n), lambda i: (i, 0)),
        out_shape=jax.ShapeDtypeStruct((m, n), BF16),
        compiler_params=_params("parallel"),
        name="mla_dq",
    )(xb, w, g)


def _kv_kernel(x_ref, w_ref, g_ref, cs_ref, c_ref, cb_ref, kr_ref, krz_ref, *, lora):
    y = jnp.dot(x_ref[...], w_ref[...], preferred_element_type=F32)
    c = y[:, :lora]
    ms = jnp.mean(c * c, axis=-1, keepdims=True)
    c = c * lax.rsqrt(ms + RMS_EPS) * g_ref[...]
    c_ref[...] = c
    cb_ref[...] = c.astype(BF16)
    t = y[:, lora:lora + 2 * QK_ROPE] * cs_ref[...]
    kr = t + pltpu.roll(t, QK_ROPE, 1)
    kr_ref[...] = kr[:, :QK_ROPE]
    lane = lax.broadcasted_iota(jnp.int32, kr.shape, 1)
    krz_ref[...] = jnp.where(lane < QK_ROPE, kr, 0.0).astype(BF16)


def _shared_kv(xb, w, g, cs):
    m, k = xb.shape
    n = w.shape[1]
    lora = n - 2 * QK_ROPE
    bm = _blk(m, 512, 8)
    return pl.pallas_call(
        functools.partial(_kv_kernel, lora=lora),
        grid=(m // bm,),
        in_specs=[pl.BlockSpec((bm, k), lambda i: (i, 0)),
                  pl.BlockSpec((k, n), lambda i: (0, 0)),
                  pl.BlockSpec((1, lora), lambda i: (0, 0)),
                  pl.BlockSpec((bm, 2 * QK_ROPE), lambda i: (i, 0))],
        out_specs=[pl.BlockSpec((bm, lora), lambda i: (i, 0)),
                   pl.BlockSpec((bm, lora), lambda i: (i, 0)),
                   pl.BlockSpec((bm, QK_ROPE), lambda i: (i, 0)),
                   pl.BlockSpec((bm, LANES), lambda i: (i, 0))],
        out_shape=[jax.ShapeDtypeStruct((m, lora), F32), jax.ShapeDtypeStruct((m, lora), BF16),
                   jax.ShapeDtypeStruct((m, QK_ROPE), F32), jax.ShapeDtypeStruct((m, LANES), BF16)],
        compiler_params=_params("parallel"),
        name="mla_shared_kv",
    )(xb, w, g, cs)


def _uq_kernel(x_ref, w_ref, cs_ref, o_ref, *, hb, scale):
    y = jnp.dot(x_ref[...], w_ref[...], preferred_element_type=F32)
    cs = cs_ref[...]
    for h in range(hb):
        lo = h * HEAD_EXT
        o_ref[:, lo:lo + QK_NOPE] = (y[:, lo:lo + QK_NOPE] * scale).astype(BF16)
        t = y[:, lo + QK_NOPE:lo + HEAD_EXT] * cs
        r = (t + pltpu.roll(t, QK_ROPE, 1)) * scale
        o_ref[:, lo + QK_NOPE:lo + HEAD_EXT] = r.astype(BF16)


def _uq(cq, w_ext, cs, scale):
    m, k = cq.shape
    n = w_ext.shape[1]
    bm, bn = _blk(m, 1024, 8), _blk(n, 1024, HEAD_EXT)
    return pl.pallas_call(
        functools.partial(_uq_kernel, hb=bn // HEAD_EXT, scale=scale),
        grid=(m // bm, n // bn),
        in_specs=[pl.BlockSpec((bm, k), lambda i, j: (i, 0)),
                  pl.BlockSpec((k, bn), lambda i, j: (0, j)),
                  pl.BlockSpec((bm, 2 * QK_ROPE), lambda i, j: (i, 0))],
        out_specs=pl.BlockSpec((bm, bn), lambda i, j: (i, j)),
        out_shape=jax.ShapeDtypeStruct((m, n), BF16),
        compiler_params=_params("parallel", "arbitrary"),
        name="mla_uq",
    )(cq, w_ext, cs)


def _prompt_attn_kernel(q_ref, c_ref, krz_ref, wuk_ref, wuvt_ref, o_ref, kext_ref, vt_ref, *, s, tq, hp):
    c = c_ref[...]
    kn = jnp.dot(c, wuk_ref[...], preferred_element_type=F32).astype(BF16)
    krz = krz_ref[...]
    for h in range(hp):
        kext_ref[h, :, :QK_NOPE] = kn[:, h * QK_NOPE:(h + 1) * QK_NOPE]
        kext_ref[h, :, QK_NOPE:] = krz
    vt_ref[...] = lax.dot_general(wuvt_ref[...], c, _NT, preferred_element_type=F32).astype(BF16)
    for h in range(hp):
        for qi in range(s // tq):
            q = q_ref[qi * tq:(qi + 1) * tq, h * HEAD_EXT:(h + 1) * HEAD_EXT]
            m = jnp.full((1, tq), -jnp.inf, F32)
            l = jnp.zeros((1, tq), F32)
            acc = jnp.zeros((V_DIM, tq), F32)
            for kj in range(qi + 1):
                k = kext_ref[h, kj * tq:(kj + 1) * tq, :]
                st = lax.dot_general(k, q, _NT, preferred_element_type=F32)
                if kj == qi:
                    key = lax.broadcasted_iota(jnp.int32, (tq, tq), 0)
                    qry = lax.broadcasted_iota(jnp.int32, (tq, tq), 1)
                    st = jnp.where(key <= qry, st, -jnp.inf)
                m_new = jnp.maximum(m, jnp.max(st, axis=0, keepdims=True))
                p = jnp.exp2(st - m_new)
                a = jnp.exp2(m - m_new)
                l = a * l + jnp.sum(p, axis=0, keepdims=True)
                acc = a * acc + jnp.dot(vt_ref[h * V_DIM:(h + 1) * V_DIM, kj * tq:(kj + 1) * tq],
                                        p.astype(BF16), preferred_element_type=F32)
                m = m_new
            o_ref[qi * tq:(qi + 1) * tq, h * V_DIM:(h + 1) * V_DIM] = (acc / l).T.astype(BF16)


def _prompt_attn(q_ext, cb, krz, wuk2d, wuvt, nb, s):
    h = wuk2d.shape[1] // QK_NOPE
    lora = cb.shape[1]
    tq = _blk(s, 1024)
    hp = 2 if h % 2 == 0 else 1
    return pl.pallas_call(
        functools.partial(_prompt_attn_kernel, s=s, tq=tq, hp=hp),
        grid=(nb, h // hp),
        in_specs=[pl.BlockSpec((s, hp * HEAD_EXT), lambda b, hh: (b, hh)),
                  pl.BlockSpec((s, lora), lambda b, hh: (b, 0)),
                  pl.BlockSpec((s, LANES), lambda b, hh: (b, 0)),
                  pl.BlockSpec((lora, hp * QK_NOPE), lambda b, hh: (0, hh)),
                  pl.BlockSpec((hp * V_DIM, lora), lambda b, hh: (hh, 0))],
        out_specs=pl.BlockSpec((s, hp * V_DIM), lambda b, hh: (b, hh)),
        out_shape=jax.ShapeDtypeStruct((nb * s, h * V_DIM), BF16),
        scratch_shapes=[pltpu.VMEM((hp, s, HEAD_EXT), BF16), pltpu.VMEM((hp * V_DIM, s), BF16)],
        compiler_params=_params("parallel", "arbitrary"),
        name="mla_prompt_attn",
    )(q_ext, cb, krz, wuk2d, wuvt)


def _qlat_kernel(q_ref, wuk_ref, ql_ref, qr_ref, *, heads, tok):
    bb = ql_ref.shape[0]
    for hp in range(heads // 2):
        ql, qr = [], []
        for h in (2 * hp, 2 * hp + 1):
            qn = q_ref[:, h * HEAD_EXT:h * HEAD_EXT + QK_NOPE]
            y = lax.dot_general(qn, wuk_ref[:, h * QK_NOPE:(h + 1) * QK_NOPE], _NT,
                                preferred_element_type=F32)
            ql.append(y.reshape(bb, tok, y.shape[1]))
            r = q_ref[:, h * HEAD_EXT + QK_NOPE:(h + 1) * HEAD_EXT].astype(F32)
            qr.append(r.reshape(bb, tok, LANES))
        lo = 2 * hp * tok
        ql_ref[:, lo:lo + 2 * tok, :] = jnp.concatenate(ql, axis=1).astype(BF16)
        qr_ref[:, lo:lo + 2 * tok, :] = jnp.concatenate(qr, axis=1).astype(BF16)


def _q_latent(q_ext, wuk2d, bd, tok):
    lora = wuk2d.shape[0]
    heads = wuk2d.shape[1] // QK_NOPE
    assert heads % 2 == 0 and tok == SUBLANES
    bb = _blk(bd, 16, 1)
    return pl.pallas_call(
        functools.partial(_qlat_kernel, heads=heads, tok=tok),
        grid=(bd // bb,),
        in_specs=[pl.BlockSpec((bb * tok, heads * HEAD_EXT), lambda i: (i, 0)),
                  pl.BlockSpec((lora, heads * QK_NOPE), lambda i: (0, 0))],
        out_specs=[pl.BlockSpec((bb, heads * tok, lora), lambda i: (i, 0, 0)),
                   pl.BlockSpec((bb, heads * tok, LANES), lambda i: (i, 0, 0))],
        out_shape=[jax.ShapeDtypeStruct((bd, heads * tok, lora), BF16),
                   jax.ShapeDtypeStruct((bd, heads * tok, LANES), BF16)],
        compiler_params=_params("parallel"),
        name="mla_q_latent",
    )(q_ext, wuk2d)


def _sample_attn_kernel(pt_ref, ql_ref, qr_ref, cn_ref, rnt_ref, ckv_hbm, krt_hbm, o_ref,
                        cbuf, rbuf, kc_ref, krt_ref, s_ref, m_ref, l_ref, acc_ref, sem_c, sem_r,
                        *, pg, nsteps, ps, tok, tks):
    step = pl.program_id(1)
    n = pl.program_id(0) * nsteps + step
    total = pl.num_programs(0) * nsteps
    slot = n % 2
    ql = ql_ref[0]
    qr = qr_ref[0]
    rows = ql.shape[0]

    def page_copies(page, slot_, i):
        return (pltpu.make_async_copy(ckv_hbm.at[page], cbuf.at[slot_, i], sem_c.at[slot_]),
                pltpu.make_async_copy(krt_hbm.at[page], rbuf.at[slot_, i], sem_r.at[slot_]))

    def start_group(n_, slot_):
        for i in range(pg):
            for cp in page_copies(pt_ref[n_ * pg + i], slot_, i):
                cp.start()

    @pl.when(n == 0)
    def _():
        start_group(0, 0)

    for i in range(pg):
        for cp in page_copies(0, slot, i):
            cp.wait()

    @pl.when(n + 1 < total)
    def _():
        start_group(n + 1, 1 - slot)

    @pl.when(step == 0)
    def _():
        cn = cn_ref[0]
        s = (lax.dot_general(ql, cn, _NT, preferred_element_type=F32)
             + jnp.dot(qr, rnt_ref[0], preferred_element_type=F32))
        nk = s.shape[1]
        row = lax.broadcasted_iota(jnp.int32, (rows, nk), 0)
        col = lax.broadcasted_iota(jnp.int32, (rows, nk), 1)
        s = jnp.where(col <= (row & (tok - 1)), s, -jnp.inf)
        m = jnp.max(s, axis=1, keepdims=True)
        p = jnp.exp2(s - m)
        m_ref[...] = jnp.broadcast_to(m, m_ref.shape)
        l_ref[...] = jnp.broadcast_to(jnp.sum(p, axis=1, keepdims=True), l_ref.shape)
        acc_ref[...] = jnp.dot(p.astype(BF16), cn, preferred_element_type=F32)
        krt_ref[QK_ROPE:, :] = jnp.zeros((LANES - QK_ROPE, krt_ref.shape[1]), BF16)

    nch = pg * ps // tks
    ppc = tks // ps

    def scores(c):
        for i in range(c * ppc, (c + 1) * ppc):
            kc_ref[i * ps:(i + 1) * ps, :] = cbuf[slot, i].astype(BF16)
            krt_ref[:QK_ROPE, i * ps:(i + 1) * ps] = rbuf[slot, i].astype(BF16)
        kc = kc_ref[c * tks:(c + 1) * tks, :]
        s_ref[c % 2] = (lax.dot_general(ql, kc, _NT, preferred_element_type=F32)
                        + jnp.dot(qr, krt_ref[:, c * tks:(c + 1) * tks], preferred_element_type=F32))

    def softmax_pv(c):
        s = s_ref[c % 2]
        m_prev = m_ref[...]
        m_new = jnp.maximum(m_prev, jnp.max(s, axis=1, keepdims=True))
        a = jnp.exp2(m_prev - m_new)
        p = jnp.exp2(s - m_new[:, :1])
        l_ref[...] = a * l_ref[...] + jnp.sum(p, axis=1, keepdims=True)
        acc_ref[...] = acc_ref[...] * a[:, :1] + jnp.dot(p.astype(BF16), kc_ref[c * tks:(c + 1) * tks, :],
                                                          preferred_element_type=F32)
        m_ref[...] = m_new

    scores(0)
    for c in range(nch):
        if c + 1 < nch:
            scores(c + 1)
        softmax_pv(c)

    @pl.when(step == nsteps - 1)
    def _():
        o_ref[0] = (acc_ref[...] / l_ref[...][:, :1]).astype(BF16)


def _sample_attn(page_table, ql3, qr3, cn_pad, rnt_pad, cache_ckv, cache_krt, tok):
    bd, rows, lora = ql3.shape
    npages = page_table.shape[1]
    ps = cache_ckv.shape[1]
    pg = _blk(npages, 32, 1)
    nsteps = npages // pg
    tks = _blk(pg * ps, 2048)
    nk = cn_pad.shape[1]
    assert tok & (tok - 1) == 0
    pt_flat = page_table.reshape(-1)
    in_specs = [pl.BlockSpec((1, rows, lora), lambda b, st, pt: (b, 0, 0)),
                pl.BlockSpec((1, rows, LANES), lambda b, st, pt: (b, 0, 0)),
                pl.BlockSpec((1, nk, lora), lambda b, st, pt: (b, 0, 0)),
                pl.BlockSpec((1, LANES, nk), lambda b, st, pt: (b, 0, 0)),
                pl.BlockSpec(memory_space=pl.ANY),
                pl.BlockSpec(memory_space=pl.ANY)]
    kern = functools.partial(_sample_attn_kernel, pg=pg, nsteps=nsteps, ps=ps, tok=tok, tks=tks)
    return pl.pallas_call(
        kern,
        grid_spec=pltpu.PrefetchScalarGridSpec(
            num_scalar_prefetch=1,
            grid=(bd, nsteps),
            in_specs=in_specs,
            out_specs=pl.BlockSpec((1, rows, lora), lambda b, st, pt: (b, 0, 0)),
            scratch_shapes=[pltpu.VMEM((2, pg, ps, lora), F32), pltpu.VMEM((2, pg, QK_ROPE, ps), F32),
                            pltpu.VMEM((pg * ps, lora), BF16), pltpu.VMEM((LANES, pg * ps), BF16),
                            pltpu.VMEM((2, rows, tks), F32),
                            pltpu.VMEM((rows, LANES), F32), pltpu.VMEM((rows, LANES), F32),
                            pltpu.VMEM((rows, lora), F32),
                            pltpu.SemaphoreType.DMA((2,)), pltpu.SemaphoreType.DMA((2,))]),
        out_shape=jax.ShapeDtypeStruct((bd, rows, lora), BF16),
        compiler_params=_params("arbitrary", "arbitrary"),
        name="mla_sample_attn",
    )(pt_flat, ql3, qr3, cn_pad, rnt_pad, cache_ckv, cache_krt)


def _uv_kernel(o_ref, w_ref, y_ref, *, heads, tok):
    bb = o_ref.shape[0]
    for hp in range(heads // 2):
        x = o_ref[:, 2 * hp * tok:2 * (hp + 1) * tok, :]
        y = jnp.dot(x.reshape(bb * 2 * tok, x.shape[2]), w_ref[:, 2 * hp * V_DIM:2 * (hp + 1) * V_DIM],
                    preferred_element_type=F32).reshape(bb, 2 * tok, 2 * V_DIM)
        lo = 2 * hp * V_DIM
        y_ref[:, lo:lo + V_DIM] = y[:, :tok, :V_DIM].reshape(bb * tok, V_DIM).astype(BF16)
        y_ref[:, lo + V_DIM:lo + 2 * V_DIM] = y[:, tok:, V_DIM:].reshape(bb * tok, V_DIM).astype(BF16)


def _latent_to_v(o_lat, wuv2d, tok):
    bd, rows, lora = o_lat.shape
    heads = rows // tok
    assert heads % 2 == 0 and tok == SUBLANES
    bb = _blk(bd, 16, 1)
    return pl.pallas_call(
        functools.partial(_uv_kernel, heads=heads, tok=tok),
        grid=(bd // bb,),
        in_specs=[pl.BlockSpec((bb, rows, lora), lambda i: (i, 0, 0)),
                  pl.BlockSpec((lora, heads * V_DIM), lambda i: (0, 0))],
        out_specs=pl.BlockSpec((bb * tok, heads * V_DIM), lambda i: (i, 0)),
        out_shape=jax.ShapeDtypeStruct((bd * tok, heads * V_DIM), BF16),
        compiler_params=_params("parallel"),
        name="mla_latent_to_v",
    )(o_lat, wuv2d)


def _rope_table(pos):
    half = QK_ROPE // 2
    inv = 1.0 / (ROPE_THETA ** (jnp.arange(half, dtype=F32) / half))
    ang = pos.astype(F32)[:, None] * inv[None, :]
    cos, sin = jnp.cos(ang), jnp.sin(ang)
    return jnp.concatenate([cos, cos, -sin, sin], axis=1)


def _swap_halves(w):
    half = QK_ROPE // 2
    return jnp.concatenate([w[..., half:], w[..., :half]], axis=-1)


def kernel(x_prompt, x_sample, state_conv, cache_ckv, cache_krope, page_table, ln_g, ln_b, conv_w_pw1, conv_b_pw1, conv_w_dw, conv_b_dw, conv_ln_g, conv_ln_b, conv_w_pw2, conv_b_pw2, mla_w_dq, mla_q_norm, mla_w_uq, mla_w_o, kv_w_dkv, kv_norm, kv_w_kr, kv_w_uk, kv_w_uv, ffn_w_gate, ffn_w_up, ffn_w_down):
    nb, s, d = x_prompt.shape
    bd, t, _ = x_sample.shape
    depth = ln_g.shape[0]
    assert depth == 2 and conv_w_pw1.shape[0] == 1 and mla_w_dq.shape[0] == 1
    heads = mla_w_uq.shape[2]
    lora = kv_w_dkv.shape[1]
    ps = cache_ckv.shape[1]
    past = page_table.shape[1] * ps
    kw = conv_w_dw.shape[1]
    alpha = (2.0 * depth) ** 0.25
    q_scale = float(QK_NOPE + QK_ROPE) ** -0.5 * math.log2(math.e)

    def vec(v):
        return v.reshape(1, -1)

    w_pw1 = conv_w_pw1[0].astype(BF16)
    w_pw2 = conv_w_pw2.astype(BF16)
    wd3 = ffn_w_down.astype(BF16)
    w_kv = jnp.concatenate([kv_w_dkv, kv_w_kr, _swap_halves(kv_w_kr)], axis=1).astype(BF16)
    w_dq = mla_w_dq[0].astype(BF16)
    w_uq = mla_w_uq[0]
    w_uq_ext = jnp.concatenate([w_uq, _swap_halves(w_uq[..., QK_NOPE:])], axis=-1)
    w_uq_ext = w_uq_ext.reshape(w_uq.shape[0], heads * HEAD_EXT).astype(BF16)
    wuk2d = kv_w_uk.reshape(lora, heads * QK_NOPE).astype(BF16)
    wuv2d = kv_w_uv.reshape(lora, heads * V_DIM).astype(BF16)
    wuvt = jnp.transpose(kv_w_uv, (1, 2, 0)).reshape(heads * V_DIM, lora).astype(BF16)
    w_o3 = mla_w_o.reshape(1, heads * V_DIM, d).astype(BF16)
    conv_args = (conv_w_dw[0], vec(conv_b_dw[0]), vec(conv_ln_g[0]), vec(conv_ln_b[0]))

    def ffn(h, hb, l):
        mid = _gate_up(hb, ffn_w_gate, ffn_w_up, l)
        pre = _mm_residual(mid, wd3, l, h, None, alpha, ffn_w_down.shape[1] // 2)
        return _ln(pre, vec(ln_g[l, 1]), vec(ln_b[l, 1]))

    def layer0(x, conv):
        g = _pw1_glu(x, w_pw1, vec(conv_b_pw1[0]))
        z, extra = conv(g)
        pre = _mm_residual(z, w_pw2, 0, x, vec(conv_b_pw2[0]), alpha, d)
        h1, h1b = _ln(pre, vec(ln_g[0, 0]), vec(ln_b[0, 0]))
        h2, h2b = ffn(h1, h1b, 0)
        return h2, h2b, extra

    def qkv(h2b, cs):
        c, cb, kr, krz = _shared_kv(h2b, w_kv, vec(kv_norm), cs)
        cq = _dq(h2b, w_dq, vec(mla_q_norm[0]))
        return c, cb, kr, krz, _uq(cq, w_uq_ext, cs, q_scale)

    def layer1_tail(o, h2):
        pre = _mm_residual(o, w_o3, 0, h2, None, alpha, 4096)
        h3, h3b = _ln(pre, vec(ln_g[1, 0]), vec(ln_b[1, 0]))
        return ffn(h3, h3b, 1)[0]

    def conv_p(g):
        g3 = g.reshape(nb, s, d)
        return _conv_prompt(g3, *conv_args).reshape(nb * s, d), g3[:, s - (kw - 1):]

    h2, h2b, conv_prompt = layer0(x_prompt.reshape(nb * s, d), conv_p)
    cs_p = jnp.tile(_rope_table(jnp.arange(s)), (nb, 1))
    ckv_p, cb_p, kr_p, krz_p, q_p = qkv(h2b, cs_p)
    o_p = _prompt_attn(q_p, cb_p, krz_p, wuk2d, wuvt, nb, s)
    y_p = layer1_tail(o_p, h2)

    def conv_s(g):
        z, new_state = _conv_sample(state_conv[0], g.reshape(bd, t, d), *conv_args)
        return z.reshape(bd * t, d), new_state

    g2, g2b, conv_sample = layer0(x_sample.reshape(bd * t, d), conv_s)
    cs_s = jnp.tile(_rope_table(past + jnp.arange(t)), (bd, 1))
    ckv_s, cb_s, kr_s, krz_s, q_s = qkv(g2b, cs_s)
    ql3, qr3 = _q_latent(q_s, wuk2d, bd, t)
    nk = LANES
    cn_pad = jnp.pad(cb_s.reshape(bd, t, lora), ((0, 0), (0, nk - t), (0, 0)))
    rnt_pad = jnp.pad(jnp.swapaxes(krz_s.reshape(bd, t, LANES), 1, 2), ((0, 0), (0, 0), (0, nk - t)))
    cache_krt = jnp.swapaxes(cache_krope, 1, 2)
    o_lat = _sample_attn(page_table, ql3, qr3, cn_pad, rnt_pad, cache_ckv, cache_krt, t)
    o_s = _latent_to_v(o_lat, wuv2d, t)
    y_s = layer1_tail(o_s, g2)

    return (y_p.reshape(nb, s, d), y_s.reshape(bd, t, d), conv_prompt[None], conv_sample[None],
            ckv_p.reshape(nb, s, lora), ckv_s.reshape(bd, t, lora),
            kr_p.reshape(nb, s, QK_ROPE), kr_s.reshape(bd, t, QK_ROPE))
```

```python
import functools
import math

import jax
import jax.numpy as jnp
from jax import lax
from jax.experimental import pallas as pl
from jax.experimental.pallas import tpu as pltpu

F32 = jnp.float32
BF16 = jnp.bfloat16

LN_EPS = 1e-5
RMS_EPS = 1e-6
ROPE_THETA = 10000.0
QK_NOPE = 128
QK_ROPE = 64
V_DIM = 128
HEAD_EXT = 256
LANES = 128
SUBLANES = 8
V7X_VMEM_LIMIT = 56 * 1024 * 1024

_NT = (((1,), (1,)), ((), ()))


def _params(*sem):
    return pltpu.CompilerParams(dimension_semantics=sem, vmem_limit_bytes=V7X_VMEM_LIMIT)


def _blk(dim, pref, unit=LANES):
    if dim <= pref:
        return dim
    b = (pref // unit) * unit
    while b > unit and dim % b:
        b -= unit
    assert dim % b == 0, (dim, pref, unit)
    return b


def _layer_norm(x, g, b):
    mu = jnp.mean(x, axis=-1, keepdims=True)
    xc = x - mu
    var = jnp.mean(xc * xc, axis=-1, keepdims=True)
    return xc * lax.rsqrt(var + LN_EPS) * g + b


def _glu_kernel(x_ref, wa_ref, wb_ref, ba_ref, bb_ref, o_ref):
    x = x_ref[...].astype(BF16)
    a = jnp.dot(x, wa_ref[...].astype(BF16), preferred_element_type=F32) + ba_ref[...]
    b = jnp.dot(x, wb_ref[...].astype(BF16), preferred_element_type=F32) + bb_ref[...]
    o_ref[...] = a * jax.nn.sigmoid(b)


def _pw1_glu(x, w, bias):
    m, k = x.shape
    d = w.shape[1] // 2
    bm, bn = _blk(m, 1024, 8), _blk(d, 256)
    nj = d // bn
    return pl.pallas_call(
        _glu_kernel,
        grid=(m // bm, nj),
        in_specs=[
            pl.BlockSpec((bm, k), lambda i, j: (i, 0)),
            pl.BlockSpec((k, bn), lambda i, j: (0, j)),
            pl.BlockSpec((k, bn), lambda i, j: (0, j + nj)),
            pl.BlockSpec((1, bn), lambda i, j: (0, j)),
            pl.BlockSpec((1, bn), lambda i, j: (0, j + nj)),
        ],
        out_specs=pl.BlockSpec((bm, bn), lambda i, j: (i, j)),
        out_shape=jax.ShapeDtypeStruct((m, d), F32),
        compiler_params=_params("parallel", "arbitrary"),
        name="pw1_glu",
    )(x, w, w, bias, bias)


def _conv_prompt_kernel(g_ref, halo_ref, w_ref, bdw_ref, lg_ref, lb_ref, z_ref, win_ref, sh_ref, y_ref,
                        *, tt, kw, halo, rt, cw):
    i = pl.program_id(1)
    d = y_ref.shape[1]
    keep = jnp.where(i == 0, 0.0, 1.0).astype(F32)
    win_ref[0:halo, :] = halo_ref[0] * keep
    win_ref[halo:halo + tt, :] = g_ref[0]
    base = halo - (kw - 1)
    nrow = sh_ref.shape[1]

    def chunk(c, carry):
        off = pl.multiple_of(c * cw, cw)
        for r in range(1, SUBLANES):
            sh_ref[r - 1] = win_ref[pl.ds(r, nrow), pl.ds(off, cw)]
        for rb in range(tt // rt):
            acc = jnp.zeros((rt, cw), F32)
            for k in range(kw):
                s = k + base
                r = s % SUBLANES
                a = rb * rt + s - r
                if r == 0:
                    x = win_ref[pl.ds(a, rt), pl.ds(off, cw)]
                else:
                    x = sh_ref[r - 1, pl.ds(a, rt), :]
                acc = acc + w_ref[k:k + 1, pl.ds(off, cw)] * x
            y_ref[rb * rt:(rb + 1) * rt, pl.ds(off, cw)] = acc + bdw_ref[:, pl.ds(off, cw)]
        return carry

    lax.fori_loop(0, d // cw, chunk, 0)
    y = _layer_norm(y_ref[...], lg_ref[...], lb_ref[...])
    z_ref[0] = (y * jax.nn.sigmoid(y)).astype(BF16)


def _conv_prompt(g3, w_dw, b_dw, ln_g, ln_b):
    b, s, d = g3.shape
    kw = w_dw.shape[0]
    halo = 32
    assert kw - 1 <= halo
    tt = _blk(s, 256, halo)
    rt, cw = _blk(tt, 128, SUBLANES), LANES
    hb = tt // halo
    kern = functools.partial(_conv_prompt_kernel, tt=tt, kw=kw, halo=halo, rt=rt, cw=cw)
    return pl.pallas_call(
        kern,
        grid=(b, s // tt),
        in_specs=[
            pl.BlockSpec((1, tt, d), lambda bi, i: (bi, i, 0)),
            pl.BlockSpec((1, halo, d), lambda bi, i: (bi, jnp.maximum(i * hb - 1, 0), 0)),
            pl.BlockSpec((kw, d), lambda bi, i: (0, 0)),
            pl.BlockSpec((1, d), lambda bi, i: (0, 0)),
            pl.BlockSpec((1, d), lambda bi, i: (0, 0)),
            pl.BlockSpec((1, d), lambda bi, i: (0, 0)),
        ],
        out_specs=pl.BlockSpec((1, tt, d), lambda bi, i: (bi, i, 0)),
        out_shape=jax.ShapeDtypeStruct((b, s, d), BF16),
        scratch_shapes=[pltpu.VMEM((halo + tt, d), F32),
                        pltpu.VMEM((SUBLANES - 1, halo + tt - SUBLANES, cw), F32),
                        pltpu.VMEM((tt, d), F32)],
        compiler_params=_params("parallel", "arbitrary"),
        name="conv_prompt",
    )(g3, g3, w_dw, b_dw, ln_g, ln_b)


def _conv_sample_kernel(st_ref, g_ref, w_ref, bdw_ref, lg_ref, lb_ref, z_ref, ns_ref, pad_ref, sh_ref, y_ref,
                        *, t, kw, cw):
    cs = kw - 1
    bb = g_ref.shape[0]
    d = g_ref.shape[2]
    nrow = sh_ref.shape[2]
    pad_ref[:, nrow:, :] = jnp.zeros((bb, pad_ref.shape[1] - nrow, d), F32)
    pad_ref[:, 0:cs, :] = st_ref[...]
    pad_ref[:, cs:cs + t, :] = g_ref[...]

    def chunk(c, carry):
        off = pl.multiple_of(c * cw, cw)
        for r in range(1, SUBLANES):
            sh_ref[r - 1] = pad_ref[:, pl.ds(r, nrow), pl.ds(off, cw)]
        acc = jnp.zeros((bb, t, cw), F32)
        for k in range(kw):
            r = k % SUBLANES
            a = k - r
            if r == 0:
                x = pad_ref[:, pl.ds(a, t), pl.ds(off, cw)]
            else:
                x = sh_ref[r - 1, :, pl.ds(a, t), :]
            acc = acc + w_ref[k:k + 1, pl.ds(off, cw)][None] * x
        y_ref[:, :, pl.ds(off, cw)] = acc + bdw_ref[:, pl.ds(off, cw)][None]
        return carry

    lax.fori_loop(0, d // cw, chunk, 0)
    y = _layer_norm(y_ref[...], lg_ref[...][None], lb_ref[...][None])
    z_ref[...] = (y * jax.nn.sigmoid(y)).astype(BF16)
    ns_ref[...] = pad_ref[:, t:t + cs, :]


def _conv_sample(state, g3, w_dw, b_dw, ln_g, ln_b):
    bd, t, d = g3.shape
    kw = w_dw.shape[0]
    cs = kw - 1
    assert t == SUBLANES
    bb = _blk(bd, 8, 1)
    cw = _blk(d, 512)
    nrow = -(-cs // SUBLANES) * SUBLANES
    kern = functools.partial(_conv_sample_kernel, t=t, kw=kw, cw=cw)
    return pl.pallas_call(
        kern,
        grid=(bd // bb,),
        in_specs=[
            pl.BlockSpec((bb, cs, d), lambda i: (i, 0, 0)),
            pl.BlockSpec((bb, t, d), lambda i: (i, 0, 0)),
            pl.BlockSpec((kw, d), lambda i: (0, 0)),
            pl.BlockSpec((1, d), lambda i: (0, 0)),
            pl.BlockSpec((1, d), lambda i: (0, 0)),
            pl.BlockSpec((1, d), lambda i: (0, 0)),
        ],
        out_specs=[pl.BlockSpec((bb, t, d), lambda i: (i, 0, 0)),
                   pl.BlockSpec((bb, cs, d), lambda i: (i, 0, 0))],
        out_shape=[jax.ShapeDtypeStruct((bd, t, d), BF16), jax.ShapeDtypeStruct((bd, cs, d), F32)],
        scratch_shapes=[pltpu.VMEM((bb, nrow + SUBLANES, d), F32),
                        pltpu.VMEM((SUBLANES - 1, bb, nrow, cw), F32),
                        pltpu.VMEM((bb, t, d), F32)],
        compiler_params=_params("parallel"),
        name="conv_sample",
    )(state, g3, w_dw, b_dw, ln_g, ln_b)


def _mm_res_kernel(*refs, alpha, has_bias):
    if has_bias:
        x_ref, w_ref, r_ref, b_ref, o_ref = refs
    else:
        x_ref, w_ref, r_ref, o_ref = refs
        b_ref = None
    k = pl.program_id(2)

    @pl.when(k == 0)
    def _():
        init = alpha * r_ref[...]
        if has_bias:
            init = init + b_ref[...]
        o_ref[...] = init

    o_ref[...] += jnp.dot(x_ref[...], w_ref[...], preferred_element_type=F32)


def _mm_residual(xb, w3, layer, resid, bias, alpha, tk_pref):
    m, k = xb.shape
    n = w3.shape[2]
    bm, bn, tk = _blk(m, 1024, 8), _blk(n, 512), _blk(k, tk_pref)
    has_bias = bias is not None
    in_specs = [
        pl.BlockSpec((bm, tk), lambda i, j, kk: (i, kk)),
        pl.BlockSpec((None, tk, bn), lambda i, j, kk: (layer, kk, j)),
        pl.BlockSpec((bm, bn), lambda i, j, kk: (i, j)),
    ]
    args = [xb, w3, resid]
    if has_bias:
        in_specs.append(pl.BlockSpec((1, bn), lambda i, j, kk: (0, j)))
        args.append(bias)
    return pl.pallas_call(
        functools.partial(_mm_res_kernel, alpha=alpha, has_bias=has_bias),
        grid=(m // bm, n // bn, k // tk),
        in_specs=in_specs,
        out_specs=pl.BlockSpec((bm, bn), lambda i, j, kk: (i, j)),
        out_shape=jax.ShapeDtypeStruct((m, n), F32),
        compiler_params=_params("parallel", "parallel", "arbitrary"),
        name="mm_residual",
    )(*args)


def _ln_kernel(x_ref, g_ref, b_ref, o_ref, ob_ref):
    y = _layer_norm(x_ref[...], g_ref[...], b_ref[...])
    o_ref[...] = y
    ob_ref[...] = y.astype(BF16)


def _ln(x, g, b):
    m, d = x.shape
    bm = _blk(m, 256, 8)
    return pl.pallas_call(
        _ln_kernel,
        grid=(m // bm,),
        in_specs=[pl.BlockSpec((bm, d), lambda i: (i, 0)),
                  pl.BlockSpec((1, d), lambda i: (0, 0)),
                  pl.BlockSpec((1, d), lambda i: (0, 0))],
        out_specs=[pl.BlockSpec((bm, d), lambda i: (i, 0)), pl.BlockSpec((bm, d), lambda i: (i, 0))],
        out_shape=[jax.ShapeDtypeStruct((m, d), F32), jax.ShapeDtypeStruct((m, d), BF16)],
        compiler_params=_params("parallel"),
        name="layer_norm",
    )(x, g, b)


def _gate_up_kernel(x_ref, wg_ref, wu_ref, o_ref):
    x = x_ref[...]
    a = jnp.dot(x, wg_ref[...].astype(BF16), preferred_element_type=F32)
    b = jnp.dot(x, wu_ref[...].astype(BF16), preferred_element_type=F32)
    o_ref[...] = (a * jax.nn.sigmoid(a) * b).astype(BF16)


def _gate_up(xb, wg3, wu3, layer):
    m, k = xb.shape
    n = wg3.shape[2]
    bm, bn = _blk(m, 2048, 8), _blk(n, 256)
    return pl.pallas_call(
        _gate_up_kernel,
        grid=(m // bm, n // bn),
        in_specs=[pl.BlockSpec((bm, k), lambda i, j: (i, 0)),
                  pl.BlockSpec((None, k, bn), lambda i, j: (layer, 0, j)),
                  pl.BlockSpec((None, k, bn), lambda i, j: (layer, 0, j))],
        out_specs=pl.BlockSpec((bm, bn), lambda i, j: (i, j)),
        out_shape=jax.ShapeDtypeStruct((m, n), BF16),
        compiler_params=_params("parallel", "arbitrary"),
        name="ffn_gate_up",
    )(xb, wg3, wu3)


def _dq_kernel(x_ref, w_ref, g_ref, o_ref):
    y = jnp.dot(x_ref[...], w_ref[...], preferred_element_type=F32)
    ms = jnp.mean(y * y, axis=-1, keepdims=True)
    o_ref[...] = (y * lax.rsqrt(ms + RMS_EPS) * g_ref[...]).astype(BF16)


def _dq(xb, w, g):
    m, k = xb.shape
    n = w.shape[1]
    bm = _blk(m, 512, 8)
    return pl.pallas_call(
        _dq_kernel,
        grid=(m // bm,),
        in_specs=[pl.BlockSpec((bm, k), lambda i: (i, 0)),
                  pl.BlockSpec((k, n), lambda i: (0, 0)),
                  pl.BlockSpec((1, n), lambda i: (0, 0))],
        out_specs=pl.BlockSpec((bm, n), lambda i: (i, 0)),
        out_shape=jax.ShapeDtypeStruct((m, n), BF16),
        compiler_params=_params("parallel"),
        name="mla_dq",
    )(xb, w, g)


def _kv_kernel(x_ref, w_ref, g_ref, cs_ref, c_ref, cb_ref, kr_ref, krz_ref, *, lora):
    y = jnp.dot(x_ref[...], w_ref[...], preferred_element_type=F32)
    c = y[:, :lora]
    ms = jnp.mean(c * c, axis=-1, keepdims=True)
    c = c * lax.rsqrt(ms + RMS_EPS) * g_ref[...]
    c_ref[...] = c
    cb_ref[...] = c.astype(BF16)
    t = y[:, lora:lora + 2 * QK_ROPE] * cs_ref[...]
    kr = t + pltpu.roll(t, QK_ROPE, 1)
    kr_ref[...] = kr[:, :QK_ROPE]
    lane = lax.broadcasted_iota(jnp.int32, kr.shape, 1)
    krz_ref[...] = jnp.where(lane < QK_ROPE, kr, 0.0).astype(BF16)


def _shared_kv(xb, w, g, cs):
    m, k = xb.shape
    n = w.shape[1]
    lora = n - 2 * QK_ROPE
    bm = _blk(m, 512, 8)
    return pl.pallas_call(
        functools.partial(_kv_kernel, lora=lora),
        grid=(m // bm,),
        in_specs=[pl.BlockSpec((bm, k), lambda i: (i, 0)),
                  pl.BlockSpec((k, n), lambda i: (0, 0)),
                  pl.BlockSpec((1, lora), lambda i: (0, 0)),
                  pl.BlockSpec((bm, 2 * QK_ROPE), lambda i: (i, 0))],
        out_specs=[pl.BlockSpec((bm, lora), lambda i: (i, 0)),
                   pl.BlockSpec((bm, lora), lambda i: (i, 0)),
                   pl.BlockSpec((bm, QK_ROPE), lambda i: (i, 0)),
                   pl.BlockSpec((bm, LANES), lambda i: (i, 0))],
        out_shape=[jax.ShapeDtypeStruct((m, lora), F32), jax.ShapeDtypeStruct((m, lora), BF16),
                   jax.ShapeDtypeStruct((m, QK_ROPE), F32), jax.ShapeDtypeStruct((m, LANES), BF16)],
        compiler_params=_params("parallel"),
        name="mla_shared_kv",
    )(xb, w, g, cs)


def _uq_kernel(x_ref, wn_ref, wr_ref, cos_ref, sin_ref, o_ref, *, hb, scale):
    x = x_ref[...]
    yn = jnp.dot(x, wn_ref[...].astype(BF16), preferred_element_type=F32)
    yr = jnp.dot(x, wr_ref[...].astype(BF16), preferred_element_type=F32)
    cos, sin = cos_ref[...], sin_ref[...]
    lane = lax.broadcasted_iota(jnp.int32, cos.shape, 1)
    first_half = (lane & (QK_ROPE - 1)) < QK_ROPE // 2
    low = lane < QK_ROPE
    for h in range(hb):
        o_ref[:, h * HEAD_EXT:h * HEAD_EXT + QK_NOPE] = (yn[:, h * QK_NOPE:(h + 1) * QK_NOPE] * scale).astype(BF16)
    for p in range(hb // 2):
        t = yr[:, p * LANES:(p + 1) * LANES]
        sw = jnp.where(first_half, pltpu.roll(t, LANES - QK_ROPE // 2, 1), pltpu.roll(t, QK_ROPE // 2, 1))
        r = (t * cos + sw * sin) * scale
        for j, rr in enumerate((r, pltpu.roll(r, QK_ROPE, 1))):
            lo = (2 * p + j) * HEAD_EXT + QK_NOPE
            o_ref[:, lo:lo + LANES] = jnp.where(low, rr, 0.0).astype(BF16)


def _uq(cq, w_nope, w_rope, cos4, sin4, scale):
    m, k = cq.shape
    heads = w_nope.shape[1] // QK_NOPE
    hb = 4 if heads % 4 == 0 else 2
    assert heads % hb == 0
    bm = _blk(m, 1024, 8)
    return pl.pallas_call(
        functools.partial(_uq_kernel, hb=hb, scale=scale),
        grid=(m // bm, heads // hb),
        in_specs=[pl.BlockSpec((bm, k), lambda i, j: (i, 0)),
                  pl.BlockSpec((k, hb * QK_NOPE), lambda i, j: (0, j)),
                  pl.BlockSpec((k, hb * QK_ROPE), lambda i, j: (0, j)),
                  pl.BlockSpec((bm, LANES), lambda i, j: (i, 0)),
                  pl.BlockSpec((bm, LANES), lambda i, j: (i, 0))],
        out_specs=pl.BlockSpec((bm, hb * HEAD_EXT), lambda i, j: (i, j)),
        out_shape=jax.ShapeDtypeStruct((m, heads * HEAD_EXT), BF16),
        compiler_params=_params("parallel", "arbitrary"),
        name="mla_uq",
    )(cq, w_nope, w_rope, cos4, sin4)


def _prompt_attn_kernel(q_ref, c_ref, krz_ref, wuk_ref, wuvt_ref, o_ref, kext_ref, vt_ref, *, s, tq, hp):
    c = c_ref[...]
    kn = jnp.dot(c, wuk_ref[...], preferred_element_type=F32).astype(BF16)
    krz = krz_ref[...]
    for h in range(hp):
        kext_ref[h, :, :QK_NOPE] = kn[:, h * QK_NOPE:(h + 1) * QK_NOPE]
        kext_ref[h, :, QK_NOPE:] = krz
    vt_ref[...] = lax.dot_general(wuvt_ref[...], c, _NT, preferred_element_type=F32).astype(BF16)
    for h in range(hp):
        for qi in range(s // tq):
            q = q_ref[qi * tq:(qi + 1) * tq, h * HEAD_EXT:(h + 1) * HEAD_EXT]
            m = jnp.full((1, tq), -jnp.inf, F32)
            l = jnp.zeros((1, tq), F32)
            acc = jnp.zeros((V_DIM, tq), F32)
            for kj in range(qi + 1):
                k = kext_ref[h, kj * tq:(kj + 1) * tq, :]
                st = lax.dot_general(k, q, _NT, preferred_element_type=F32)
                if kj == qi:
                    key = lax.broadcasted_iota(jnp.int32, (tq, tq), 0)
                    qry = lax.broadcasted_iota(jnp.int32, (tq, tq), 1)
                    st = jnp.where(key <= qry, st, -jnp.inf)
                m_new = jnp.maximum(m, jnp.max(st, axis=0, keepdims=True))
                p = jnp.exp2(st - m_new)
                a = jnp.exp2(m - m_new)
                l = a * l + jnp.sum(p, axis=0, keepdims=True)
                acc = a * acc + jnp.dot(vt_ref[h * V_DIM:(h + 1) * V_DIM, kj * tq:(kj + 1) * tq],
                                        p.astype(BF16), preferred_element_type=F32)
                m = m_new
            o_ref[qi * tq:(qi + 1) * tq, h * V_DIM:(h + 1) * V_DIM] = (acc / l).T.astype(BF16)


def _prompt_attn(q_ext, cb, krz, wuk2d, wuvt, nb, s):
    h = wuk2d.shape[1] // QK_NOPE
    lora = cb.shape[1]
    tq = _blk(s, 1024)
    hp = 2 if h % 2 == 0 else 1
    return pl.pallas_call(
        functools.partial(_prompt_attn_kernel, s=s, tq=tq, hp=hp),
        grid=(nb, h // hp),
        in_specs=[pl.BlockSpec((s, hp * HEAD_EXT), lambda b, hh: (b, hh)),
                  pl.BlockSpec((s, lora), lambda b, hh: (b, 0)),
                  pl.BlockSpec((s, LANES), lambda b, hh: (b, 0)),
                  pl.BlockSpec((lora, hp * QK_NOPE), lambda b, hh: (0, hh)),
                  pl.BlockSpec((hp * V_DIM, lora), lambda b, hh: (hh, 0))],
        out_specs=pl.BlockSpec((s, hp * V_DIM), lambda b, hh: (b, hh)),
        out_shape=jax.ShapeDtypeStruct((nb * s, h * V_DIM), BF16),
        scratch_shapes=[pltpu.VMEM((hp, s, HEAD_EXT), BF16), pltpu.VMEM((hp * V_DIM, s), BF16)],
        compiler_params=_params("parallel", "arbitrary"),
        name="mla_prompt_attn",
    )(q_ext, cb, krz, wuk2d, wuvt)


def _qlat_kernel(q_ref, wuk_ref, ql_ref, qr_ref, *, heads, tok):
    bb = ql_ref.shape[0]
    for hp in range(heads // 2):
        ql, qr = [], []
        for h in (2 * hp, 2 * hp + 1):
            qn = q_ref[:, h * HEAD_EXT:h * HEAD_EXT + QK_NOPE]
            y = lax.dot_general(qn, wuk_ref[:, h * QK_NOPE:(h + 1) * QK_NOPE], _NT,
                                preferred_element_type=F32)
            ql.append(y.reshape(bb, tok, y.shape[1]))
            r = q_ref[:, h * HEAD_EXT + QK_NOPE:(h + 1) * HEAD_EXT].astype(F32)
            qr.append(r.reshape(bb, tok, LANES))
        lo = 2 * hp * tok
        ql_ref[:, lo:lo + 2 * tok, :] = jnp.concatenate(ql, axis=1).astype(BF16)
        qr_ref[:, lo:lo + 2 * tok, :] = jnp.concatenate(qr, axis=1).astype(BF16)


def _q_latent(q_ext, wuk2d, bd, tok):
    lora = wuk2d.shape[0]
    heads = wuk2d.shape[1] // QK_NOPE
    assert heads % 2 == 0 and tok == SUBLANES
    bb = _blk(bd, 16, 1)
    return pl.pallas_call(
        functools.partial(_qlat_kernel, heads=heads, tok=tok),
        grid=(bd // bb,),
        in_specs=[pl.BlockSpec((bb * tok, heads * HEAD_EXT), lambda i: (i, 0)),
                  pl.BlockSpec((lora, heads * QK_NOPE), lambda i: (0, 0))],
        out_specs=[pl.BlockSpec((bb, heads * tok, lora), lambda i: (i, 0, 0)),
                   pl.BlockSpec((bb, heads * tok, LANES), lambda i: (i, 0, 0))],
        out_shape=[jax.ShapeDtypeStruct((bd, heads * tok, lora), BF16),
                   jax.ShapeDtypeStruct((bd, heads * tok, LANES), BF16)],
        compiler_params=_params("parallel"),
        name="mla_q_latent",
    )(q_ext, wuk2d)


def _sample_attn_kernel(pt_ref, ql_ref, qr_ref, cn_ref, rnt_ref, ckv_hbm, krt_hbm, o_ref,
                        cbuf, rbuf, kc_ref, krt_ref, s_ref, m_ref, l_ref, acc_ref, sem_c, sem_r,
                        *, pg, nsteps, ps, tok, tks):
    step = pl.program_id(1)
    n = pl.program_id(0) * nsteps + step
    total = pl.num_programs(0) * nsteps
    slot = n % 2
    ql = ql_ref[0]
    qr = qr_ref[0]
    rows = ql.shape[0]

    def page_copies(page, slot_, i):
        return (pltpu.make_async_copy(ckv_hbm.at[page], cbuf.at[slot_, i], sem_c.at[slot_]),
                pltpu.make_async_copy(krt_hbm.at[page], rbuf.at[slot_, i], sem_r.at[slot_]))

    def start_group(n_, slot_):
        for i in range(pg):
            for cp in page_copies(pt_ref[n_ * pg + i], slot_, i):
                cp.start()

    @pl.when(n == 0)
    def _():
        start_group(0, 0)

    for i in range(pg):
        for cp in page_copies(0, slot, i):
            cp.wait()

    @pl.when(n + 1 < total)
    def _():
        start_group(n + 1, 1 - slot)

    @pl.when(step == 0)
    def _():
        cn = cn_ref[0]
        s = (lax.dot_general(ql, cn, _NT, preferred_element_type=F32)
             + jnp.dot(qr, rnt_ref[0], preferred_element_type=F32))
        nk = s.shape[1]
        row = lax.broadcasted_iota(jnp.int32, (rows, nk), 0)
        col = lax.broadcasted_iota(jnp.int32, (rows, nk), 1)
        s = jnp.where(col <= (row & (tok - 1)), s, -jnp.inf)
        m = jnp.max(s, axis=1, keepdims=True)
        p = jnp.exp2(s - m)
        m_ref[...] = jnp.broadcast_to(m, m_ref.shape)
        l_ref[...] = jnp.broadcast_to(jnp.sum(p, axis=1, keepdims=True), l_ref.shape)
        acc_ref[...] = jnp.dot(p.astype(BF16), cn, preferred_element_type=F32)
        krt_ref[QK_ROPE:, :] = jnp.zeros((LANES - QK_ROPE, krt_ref.shape[1]), BF16)

    nch = pg * ps // tks
    ppc = tks // ps

    def scores(c):
        for i in range(c * ppc, (c + 1) * ppc):
            kc_ref[i * ps:(i + 1) * ps, :] = cbuf[slot, i].astype(BF16)
            krt_ref[:QK_ROPE, i * ps:(i + 1) * ps] = rbuf[slot, i].astype(BF16)
        kc = kc_ref[c * tks:(c + 1) * tks, :]
        s_ref[c % 2] = (lax.dot_general(ql, kc, _NT, preferred_element_type=F32)
                        + jnp.dot(qr, krt_ref[:, c * tks:(c + 1) * tks], preferred_element_type=F32))

    def softmax_pv(c):
        s = s_ref[c % 2]
        m_prev = m_ref[...]
        m_new = jnp.maximum(m_prev, jnp.max(s, axis=1, keepdims=True))
        a = jnp.exp2(m_prev - m_new)
        p = jnp.exp2(s - m_new[:, :1])
        l_ref[...] = a * l_ref[...] + jnp.sum(p, axis=1, keepdims=True)
        acc_ref[...] = acc_ref[...] * a[:, :1] + jnp.dot(p.astype(BF16), kc_ref[c * tks:(c + 1) * tks, :],
                                                          preferred_element_type=F32)
        m_ref[...] = m_new

    scores(0)
    for c in range(nch):
        if c + 1 < nch:
            scores(c + 1)
        softmax_pv(c)

    @pl.when(step == nsteps - 1)
    def _():
        o_ref[0] = (acc_ref[...] / l_ref[...][:, :1]).astype(BF16)


def _sample_attn(page_table, ql3, qr3, cn_pad, rnt_pad, cache_ckv, cache_krt, tok):
    bd, rows, lora = ql3.shape
    npages = page_table.shape[1]
    ps = cache_ckv.shape[1]
    pg = _blk(npages, 32, 1)
    nsteps = npages // pg
    tks = _blk(pg * ps, 2048)
    nk = cn_pad.shape[1]
    assert tok & (tok - 1) == 0
    pt_flat = page_table.reshape(-1)
    in_specs = [pl.BlockSpec((1, rows, lora), lambda b, st, pt: (b, 0, 0)),
                pl.BlockSpec((1, rows, LANES), lambda b, st, pt: (b, 0, 0)),
                pl.BlockSpec((1, nk, lora), lambda b, st, pt: (b, 0, 0)),
                pl.BlockSpec((1, LANES, nk), lambda b, st, pt: (b, 0, 0)),
                pl.BlockSpec(memory_space=pl.ANY),
                pl.BlockSpec(memory_space=pl.ANY)]
    kern = functools.partial(_sample_attn_kernel, pg=pg, nsteps=nsteps, ps=ps, tok=tok, tks=tks)
    return pl.pallas_call(
        kern,
        grid_spec=pltpu.PrefetchScalarGridSpec(
            num_scalar_prefetch=1,
            grid=(bd, nsteps),
            in_specs=in_specs,
            out_specs=pl.BlockSpec((1, rows, lora), lambda b, st, pt: (b, 0, 0)),
            scratch_shapes=[pltpu.VMEM((2, pg, ps, lora), F32), pltpu.VMEM((2, pg, QK_ROPE, ps), F32),
                            pltpu.VMEM((pg * ps, lora), BF16), pltpu.VMEM((LANES, pg * ps), BF16),
                            pltpu.VMEM((2, rows, tks), F32),
                            pltpu.VMEM((rows, LANES), F32), pltpu.VMEM((rows, LANES), F32),
                            pltpu.VMEM((rows, lora), F32),
                            pltpu.SemaphoreType.DMA((2,)), pltpu.SemaphoreType.DMA((2,))]),
        out_shape=jax.ShapeDtypeStruct((bd, rows, lora), BF16),
        compiler_params=_params("arbitrary", "arbitrary"),
        name="mla_sample_attn",
    )(pt_flat, ql3, qr3, cn_pad, rnt_pad, cache_ckv, cache_krt)


def _uv_kernel(o_ref, w_ref, y_ref, *, heads, tok):
    bb = o_ref.shape[0]
    for hp in range(heads // 2):
        x = o_ref[:, 2 * hp * tok:2 * (hp + 1) * tok, :]
        y = jnp.dot(x.reshape(bb * 2 * tok, x.shape[2]), w_ref[:, 2 * hp * V_DIM:2 * (hp + 1) * V_DIM],
                    preferred_element_type=F32).reshape(bb, 2 * tok, 2 * V_DIM)
        lo = 2 * hp * V_DIM
        y_ref[:, lo:lo + V_DIM] = y[:, :tok, :V_DIM].reshape(bb * tok, V_DIM).astype(BF16)
        y_ref[:, lo + V_DIM:lo + 2 * V_DIM] = y[:, tok:, V_DIM:].reshape(bb * tok, V_DIM).astype(BF16)


def _latent_to_v(o_lat, wuv2d, tok):
    bd, rows, lora = o_lat.shape
    heads = rows // tok
    assert heads % 2 == 0 and tok == SUBLANES
    bb = _blk(bd, 16, 1)
    return pl.pallas_call(
        functools.partial(_uv_kernel, heads=heads, tok=tok),
        grid=(bd // bb,),
        in_specs=[pl.BlockSpec((bb, rows, lora), lambda i: (i, 0, 0)),
                  pl.BlockSpec((lora, heads * V_DIM), lambda i: (0, 0))],
        out_specs=pl.BlockSpec((bb * tok, heads * V_DIM), lambda i: (i, 0)),
        out_shape=jax.ShapeDtypeStruct((bd * tok, heads * V_DIM), BF16),
        compiler_params=_params("parallel"),
        name="mla_latent_to_v",
    )(o_lat, wuv2d)


def _rope_tables(pos):
    half = QK_ROPE // 2
    inv = 1.0 / (ROPE_THETA ** (jnp.arange(half, dtype=F32) / half))
    ang = pos.astype(F32)[:, None] * inv[None, :]
    cos, sin = jnp.cos(ang), jnp.sin(ang)
    return (jnp.concatenate([cos, cos, -sin, sin], axis=1),
            jnp.concatenate([cos, cos, cos, cos], axis=1),
            jnp.concatenate([-sin, sin, -sin, sin], axis=1))


def _swap_halves(w):
    half = QK_ROPE // 2
    return jnp.concatenate([w[..., half:], w[..., :half]], axis=-1)


def kernel(x_prompt, x_sample, state_conv, cache_ckv, cache_krope, page_table, ln_g, ln_b, conv_w_pw1, conv_b_pw1, conv_w_dw, conv_b_dw, conv_ln_g, conv_ln_b, conv_w_pw2, conv_b_pw2, mla_w_dq, mla_q_norm, mla_w_uq, mla_w_o, kv_w_dkv, kv_norm, kv_w_kr, kv_w_uk, kv_w_uv, ffn_w_gate, ffn_w_up, ffn_w_down):
    nb, s, d = x_prompt.shape
    bd, t, _ = x_sample.shape
    depth = ln_g.shape[0]
    assert depth == 2 and conv_w_pw1.shape[0] == 1 and mla_w_dq.shape[0] == 1
    heads = mla_w_uq.shape[2]
    lora = kv_w_dkv.shape[1]
    ps = cache_ckv.shape[1]
    past = page_table.shape[1] * ps
    kw = conv_w_dw.shape[1]
    alpha = (2.0 * depth) ** 0.25
    q_scale = float(QK_NOPE + QK_ROPE) ** -0.5 * math.log2(math.e)

    def vec(v):
        return v.reshape(1, -1)

    w_pw1 = conv_w_pw1[0]
    w_pw2 = conv_w_pw2.astype(BF16)
    wd3 = ffn_w_down.astype(BF16)
    w_kv = jnp.concatenate([kv_w_dkv, kv_w_kr, _swap_halves(kv_w_kr)], axis=1).astype(BF16)
    w_dq = mla_w_dq[0].astype(BF16)
    w_uq = mla_w_uq[0]
    w_uq_nope = w_uq[..., :QK_NOPE].reshape(w_uq.shape[0], heads * QK_NOPE)
    w_uq_rope = w_uq[..., QK_NOPE:].reshape(w_uq.shape[0], heads * QK_ROPE)
    wuk2d = kv_w_uk.reshape(lora, heads * QK_NOPE).astype(BF16)
    wuv2d = kv_w_uv.reshape(lora, heads * V_DIM).astype(BF16)
    wuvt = jnp.transpose(kv_w_uv, (1, 2, 0)).reshape(heads * V_DIM, lora).astype(BF16)
    w_o3 = mla_w_o.reshape(1, heads * V_DIM, d).astype(BF16)
    conv_args = (conv_w_dw[0], vec(conv_b_dw[0]), vec(conv_ln_g[0]), vec(conv_ln_b[0]))

    def ffn(h, hb, l):
        mid = _gate_up(hb, ffn_w_gate, ffn_w_up, l)
        pre = _mm_residual(mid, wd3, l, h, None, alpha, ffn_w_down.shape[1] // 2)
        return _ln(pre, vec(ln_g[l, 1]), vec(ln_b[l, 1]))

    def layer0(x, conv):
        g = _pw1_glu(x, w_pw1, vec(conv_b_pw1[0]))
        z, extra = conv(g)
        pre = _mm_residual(z, w_pw2, 0, x, vec(conv_b_pw2[0]), alpha, d)
        h1, h1b = _ln(pre, vec(ln_g[0, 0]), vec(ln_b[0, 0]))
        h2, h2b = ffn(h1, h1b, 0)
        return h2, h2b, extra

    def qkv(h2b, pos, reps):
        cs, cos4, sin4 = (jnp.tile(tb, (reps, 1)) for tb in _rope_tables(pos))
        c, cb, kr, krz = _shared_kv(h2b, w_kv, vec(kv_norm), cs)
        cq = _dq(h2b, w_dq, vec(mla_q_norm[0]))
        return c, cb, kr, krz, _uq(cq, w_uq_nope, w_uq_rope, cos4, sin4, q_scale)

    def layer1_tail(o, h2):
        pre = _mm_residual(o, w_o3, 0, h2, None, alpha, 4096)
        h3, h3b = _ln(pre, vec(ln_g[1, 0]), vec(ln_b[1, 0]))
        return ffn(h3, h3b, 1)[0]

    def conv_p(g):
        g3 = g.reshape(nb, s, d)
        return _conv_prompt(g3, *conv_args).reshape(nb * s, d), g3[:, s - (kw - 1):]

    h2, h2b, conv_prompt = layer0(x_prompt.reshape(nb * s, d), conv_p)
    ckv_p, cb_p, kr_p, krz_p, q_p = qkv(h2b, jnp.arange(s), nb)
    o_p = _prompt_attn(q_p, cb_p, krz_p, wuk2d, wuvt, nb, s)
    y_p = layer1_tail(o_p, h2)

    def conv_s(g):
        z, new_state = _conv_sample(state_conv[0], g.reshape(bd, t, d), *conv_args)
        return z.reshape(bd * t, d), new_state

    g2, g2b, conv_sample = layer0(x_sample.reshape(bd * t, d), conv_s)
    ckv_s, cb_s, kr_s, krz_s, q_s = qkv(g2b, past + jnp.arange(t), bd)
    ql3, qr3 = _q_latent(q_s, wuk2d, bd, t)
    nk = LANES
    cn_pad = jnp.pad(cb_s.reshape(bd, t, lora), ((0, 0), (0, nk - t), (0, 0)))
    rnt_pad = jnp.pad(jnp.swapaxes(krz_s.reshape(bd, t, LANES), 1, 2), ((0, 0), (0, 0), (0, nk - t)))
    cache_krt = jnp.swapaxes(cache_krope, 1, 2)
    o_lat = _sample_attn(page_table, ql3, qr3, cn_pad, rnt_pad, cache_ckv, cache_krt, t)
    o_s = _latent_to_v(o_lat, wuv2d, t)
    y_s = layer1_tail(o_s, g2)

    return (y_p.reshape(nb, s, d), y_s.reshape(bd, t, d), conv_prompt[None], conv_sample[None],
            ckv_p.reshape(nb, s, lora), ckv_s.reshape(bd, t, lora),
            kr_p.reshape(nb, s, QK_ROPE), kr_s.reshape(bd, t, QK_ROPE))
```

```python
import functools
import math

import jax
import jax.numpy as jnp
from jax import lax
from jax.experimental import pallas as pl
from jax.experimental.pallas import tpu as pltpu

F32 = jnp.float32
BF16 = jnp.bfloat16

LN_EPS = 1e-5
RMS_EPS = 1e-6
ROPE_THETA = 10000.0
QK_NOPE = 128
QK_ROPE = 64
V_DIM = 128
HEAD_EXT = 256
LANES = 128
SUBLANES = 8
V7X_VMEM_LIMIT = 56 * 1024 * 1024

_NT = (((1,), (1,)), ((), ()))


def _params(*sem):
    return pltpu.CompilerParams(dimension_semantics=sem, vmem_limit_bytes=V7X_VMEM_LIMIT)


def _blk(dim, pref, unit=LANES):
    if dim <= pref:
        return dim
    b = (pref // unit) * unit
    while b > unit and dim % b:
        b -= unit
    assert dim % b == 0, (dim, pref, unit)
    return b


def _layer_norm(x, g, b):
    mu = jnp.mean(x, axis=-1, keepdims=True)
    xc = x - mu
    var = jnp.mean(xc * xc, axis=-1, keepdims=True)
    return xc * lax.rsqrt(var + LN_EPS) * g + b


def _glu_kernel(x_ref, wa_ref, wb_ref, ba_ref, bb_ref, o_ref):
    x = x_ref[...].astype(BF16)
    a = jnp.dot(x, wa_ref[...].astype(BF16), preferred_element_type=F32) + ba_ref[...]
    b = jnp.dot(x, wb_ref[...].astype(BF16), preferred_element_type=F32) + bb_ref[...]
    o_ref[...] = a * jax.nn.sigmoid(b)


def _pw1_glu(x, w, bias):
    m, k = x.shape
    d = w.shape[1] // 2
    bm, bn = _blk(m, 1024, 8), _blk(d, 256)
    nj = d // bn
    return pl.pallas_call(
        _glu_kernel,
        grid=(m // bm, nj),
        in_specs=[
            pl.BlockSpec((bm, k), lambda i, j: (i, 0)),
            pl.BlockSpec((k, bn), lambda i, j: (0, j)),
            pl.BlockSpec((k, bn), lambda i, j: (0, j + nj)),
            pl.BlockSpec((1, bn), lambda i, j: (0, j)),
            pl.BlockSpec((1, bn), lambda i, j: (0, j + nj)),
        ],
        out_specs=pl.BlockSpec((bm, bn), lambda i, j: (i, j)),
        out_shape=jax.ShapeDtypeStruct((m, d), F32),
        compiler_params=_params("parallel", "arbitrary"),
        name="pw1_glu",
    )(x, w, w, bias, bias)


def _conv_prompt_kernel(g_ref, halo_ref, w_ref, bdw_ref, lg_ref, lb_ref, z_ref, win_ref, sh_ref, y_ref,
                        *, tt, kw, halo, rt, cw):
    i = pl.program_id(1)
    d = y_ref.shape[1]
    keep = jnp.where(i == 0, 0.0, 1.0).astype(F32)
    win_ref[0:halo, :] = halo_ref[0] * keep
    win_ref[halo:halo + tt, :] = g_ref[0]
    base = halo - (kw - 1)
    nrow = sh_ref.shape[1]

    def chunk(c, carry):
        off = pl.multiple_of(c * cw, cw)
        for r in range(1, SUBLANES):
            sh_ref[r - 1] = win_ref[pl.ds(r, nrow), pl.ds(off, cw)]
        for rb in range(tt // rt):
            acc = jnp.zeros((rt, cw), F32)
            for k in range(kw):
                s = k + base
                r = s % SUBLANES
                a = rb * rt + s - r
                if r == 0:
                    x = win_ref[pl.ds(a, rt), pl.ds(off, cw)]
                else:
                    x = sh_ref[r - 1, pl.ds(a, rt), :]
                acc = acc + w_ref[k:k + 1, pl.ds(off, cw)] * x
            y_ref[rb * rt:(rb + 1) * rt, pl.ds(off, cw)] = acc + bdw_ref[:, pl.ds(off, cw)]
        return carry

    lax.fori_loop(0, d // cw, chunk, 0)
    y = _layer_norm(y_ref[...], lg_ref[...], lb_ref[...])
    z_ref[0] = (y * jax.nn.sigmoid(y)).astype(BF16)


def _conv_prompt(g3, w_dw, b_dw, ln_g, ln_b):
    b, s, d = g3.shape
    kw = w_dw.shape[0]
    halo = 32
    assert kw - 1 <= halo
    tt = _blk(s, 256, halo)
    rt, cw = _blk(tt, 128, SUBLANES), LANES
    hb = tt // halo
    kern = functools.partial(_conv_prompt_kernel, tt=tt, kw=kw, halo=halo, rt=rt, cw=cw)
    return pl.pallas_call(
        kern,
        grid=(b, s // tt),
        in_specs=[
            pl.BlockSpec((1, tt, d), lambda bi, i: (bi, i, 0)),
            pl.BlockSpec((1, halo, d), lambda bi, i: (bi, jnp.maximum(i * hb - 1, 0), 0)),
            pl.BlockSpec((kw, d), lambda bi, i: (0, 0)),
            pl.BlockSpec((1, d), lambda bi, i: (0, 0)),
            pl.BlockSpec((1, d), lambda bi, i: (0, 0)),
            pl.BlockSpec((1, d), lambda bi, i: (0, 0)),
        ],
        out_specs=pl.BlockSpec((1, tt, d), lambda bi, i: (bi, i, 0)),
        out_shape=jax.ShapeDtypeStruct((b, s, d), BF16),
        scratch_shapes=[pltpu.VMEM((halo + tt, d), F32),
                        pltpu.VMEM((SUBLANES - 1, halo + tt - SUBLANES, cw), F32),
                        pltpu.VMEM((tt, d), F32)],
        compiler_params=_params("parallel", "arbitrary"),
        name="conv_prompt",
    )(g3, g3, w_dw, b_dw, ln_g, ln_b)


def _conv_sample_kernel(st_ref, g_ref, w_ref, bdw_ref, lg_ref, lb_ref, z_ref, ns_ref, pad_ref, sh_ref, y_ref,
                        *, t, kw, cw):
    cs = kw - 1
    bb = g_ref.shape[0]
    d = g_ref.shape[2]
    nrow = sh_ref.shape[2]
    pad_ref[:, nrow:, :] = jnp.zeros((bb, pad_ref.shape[1] - nrow, d), F32)
    pad_ref[:, 0:cs, :] = st_ref[...]
    pad_ref[:, cs:cs + t, :] = g_ref[...]

    def chunk(c, carry):
        off = pl.multiple_of(c * cw, cw)
        for r in range(1, SUBLANES):
            sh_ref[r - 1] = pad_ref[:, pl.ds(r, nrow), pl.ds(off, cw)]
        acc = jnp.zeros((bb, t, cw), F32)
        for k in range(kw):
            r = k % SUBLANES
            a = k - r
            if r == 0:
                x = pad_ref[:, pl.ds(a, t), pl.ds(off, cw)]
            else:
                x = sh_ref[r - 1, :, pl.ds(a, t), :]
            acc = acc + w_ref[k:k + 1, pl.ds(off, cw)][None] * x
        y_ref[:, :, pl.ds(off, cw)] = acc + bdw_ref[:, pl.ds(off, cw)][None]
        return carry

    lax.fori_loop(0, d // cw, chunk, 0)
    y = _layer_norm(y_ref[...], lg_ref[...][None], lb_ref[...][None])
    z_ref[...] = (y * jax.nn.sigmoid(y)).astype(BF16)
    ns_ref[...] = pad_ref[:, t:t + cs, :]


def _conv_sample(state, g3, w_dw, b_dw, ln_g, ln_b):
    bd, t, d = g3.shape
    kw = w_dw.shape[0]
    cs = kw - 1
    assert t == SUBLANES
    bb = _blk(bd, 8, 1)
    cw = _blk(d, 512)
    nrow = -(-cs // SUBLANES) * SUBLANES
    kern = functools.partial(_conv_sample_kernel, t=t, kw=kw, cw=cw)
    return pl.pallas_call(
        kern,
        grid=(bd // bb,),
        in_specs=[
            pl.BlockSpec((bb, cs, d), lambda i: (i, 0, 0)),
            pl.BlockSpec((bb, t, d), lambda i: (i, 0, 0)),
            pl.BlockSpec((kw, d), lambda i: (0, 0)),
            pl.BlockSpec((1, d), lambda i: (0, 0)),
            pl.BlockSpec((1, d), lambda i: (0, 0)),
            pl.BlockSpec((1, d), lambda i: (0, 0)),
        ],
        out_specs=[pl.BlockSpec((bb, t, d), lambda i: (i, 0, 0)),
                   pl.BlockSpec((bb, cs, d), lambda i: (i, 0, 0))],
        out_shape=[jax.ShapeDtypeStruct((bd, t, d), BF16), jax.ShapeDtypeStruct((bd, cs, d), F32)],
        scratch_shapes=[pltpu.VMEM((bb, nrow + SUBLANES, d), F32),
                        pltpu.VMEM((SUBLANES - 1, bb, nrow, cw), F32),
                        pltpu.VMEM((bb, t, d), F32)],
        compiler_params=_params("parallel"),
        name="conv_sample",
    )(state, g3, w_dw, b_dw, ln_g, ln_b)


def _mm_res_kernel(*refs, alpha, has_bias):
    if has_bias:
        x_ref, w_ref, r_ref, b_ref, o_ref = refs
    else:
        x_ref, w_ref, r_ref, o_ref = refs
        b_ref = None
    k = pl.program_id(2)

    @pl.when(k == 0)
    def _():
        init = alpha * r_ref[...]
        if has_bias:
            init = init + b_ref[...]
        o_ref[...] = init

    o_ref[...] += jnp.dot(x_ref[...], w_ref[...].astype(BF16), preferred_element_type=F32)


def _mm_residual(xb, w3, layer, resid, bias, alpha, tk_pref):
    m, k = xb.shape
    n = w3.shape[2]
    bm, bn, tk = _blk(m, 1024, 8), _blk(n, 512), _blk(k, tk_pref)
    has_bias = bias is not None
    in_specs = [
        pl.BlockSpec((bm, tk), lambda i, j, kk: (i, kk)),
        pl.BlockSpec((None, tk, bn), lambda i, j, kk: (layer, kk, j)),
        pl.BlockSpec((bm, bn), lambda i, j, kk: (i, j)),
    ]
    args = [xb, w3, resid]
    if has_bias:
        in_specs.append(pl.BlockSpec((1, bn), lambda i, j, kk: (0, j)))
        args.append(bias)
    return pl.pallas_call(
        functools.partial(_mm_res_kernel, alpha=alpha, has_bias=has_bias),
        grid=(m // bm, n // bn, k // tk),
        in_specs=in_specs,
        out_specs=pl.BlockSpec((bm, bn), lambda i, j, kk: (i, j)),
        out_shape=jax.ShapeDtypeStruct((m, n), F32),
        compiler_params=_params("parallel", "parallel", "arbitrary"),
        name="mm_residual",
    )(*args)


def _ln_kernel(x_ref, g_ref, b_ref, o_ref, ob_ref):
    y = _layer_norm(x_ref[...], g_ref[...], b_ref[...])
    o_ref[...] = y
    ob_ref[...] = y.astype(BF16)


def _ln(x, g, b):
    m, d = x.shape
    bm = _blk(m, 256, 8)
    return pl.pallas_call(
        _ln_kernel,
        grid=(m // bm,),
        in_specs=[pl.BlockSpec((bm, d), lambda i: (i, 0)),
                  pl.BlockSpec((1, d), lambda i: (0, 0)),
                  pl.BlockSpec((1, d), lambda i: (0, 0))],
        out_specs=[pl.BlockSpec((bm, d), lambda i: (i, 0)), pl.BlockSpec((bm, d), lambda i: (i, 0))],
        out_shape=[jax.ShapeDtypeStruct((m, d), F32), jax.ShapeDtypeStruct((m, d), BF16)],
        compiler_params=_params("parallel"),
        name="layer_norm",
    )(x, g, b)


def _gate_up_kernel(x_ref, wg_ref, wu_ref, o_ref):
    x = x_ref[...]
    a = jnp.dot(x, wg_ref[...].astype(BF16), preferred_element_type=F32)
    b = jnp.dot(x, wu_ref[...].astype(BF16), preferred_element_type=F32)
    o_ref[...] = (a * jax.nn.sigmoid(a) * b).astype(BF16)


def _gate_up(xb, wg3, wu3, layer):
    m, k = xb.shape
    n = wg3.shape[2]
    bm, bn = _blk(m, 2048, 8), _blk(n, 256)
    return pl.pallas_call(
        _gate_up_kernel,
        grid=(m // bm, n // bn),
        in_specs=[pl.BlockSpec((bm, k), lambda i, j: (i, 0)),
                  pl.BlockSpec((None, k, bn), lambda i, j: (layer, 0, j)),
                  pl.BlockSpec((None, k, bn), lambda i, j: (layer, 0, j))],
        out_specs=pl.BlockSpec((bm, bn), lambda i, j: (i, j)),
        out_shape=jax.ShapeDtypeStruct((m, n), BF16),
        compiler_params=_params("parallel", "arbitrary"),
        name="ffn_gate_up",
    )(xb, wg3, wu3)


def _dq_kernel(x_ref, w_ref, g_ref, o_ref):
    y = jnp.dot(x_ref[...], w_ref[...], preferred_element_type=F32)
    ms = jnp.mean(y * y, axis=-1, keepdims=True)
    o_ref[...] = (y * lax.rsqrt(ms + RMS_EPS) * g_ref[...]).astype(BF16)


def _dq(xb, w, g):
    m, k = xb.shape
    n = w.shape[1]
    bm = _blk(m, 512, 8)
    return pl.pallas_call(
        _dq_kernel,
        grid=(m // bm,),
        in_specs=[pl.BlockSpec((bm, k), lambda i: (i, 0)),
                  pl.BlockSpec((k, n), lambda i: (0, 0)),
                  pl.BlockSpec((1, n), lambda i: (0, 0))],
        out_specs=pl.BlockSpec((bm, n), lambda i: (i, 0)),
        out_shape=jax.ShapeDtypeStruct((m, n), BF16),
        compiler_params=_params("parallel"),
        name="mla_dq",
    )(xb, w, g)


def _kv_kernel(x_ref, w_ref, g_ref, cs_ref, c_ref, cb_ref, kr_ref, krz_ref, *, lora):
    y = jnp.dot(x_ref[...], w_ref[...], preferred_element_type=F32)
    c = y[:, :lora]
    ms = jnp.mean(c * c, axis=-1, keepdims=True)
    c = c * lax.rsqrt(ms + RMS_EPS) * g_ref[...]
    c_ref[...] = c
    cb_ref[...] = c.astype(BF16)
    t = y[:, lora:lora + 2 * QK_ROPE] * cs_ref[...]
    kr = t + pltpu.roll(t, QK_ROPE, 1)
    kr_ref[...] = kr[:, :QK_ROPE]
    lane = lax.broadcasted_iota(jnp.int32, kr.shape, 1)
    krz_ref[...] = jnp.where(lane < QK_ROPE, kr, 0.0).astype(BF16)


def _shared_kv(xb, w, g, cs):
    m, k = xb.shape
    n = w.shape[1]
    lora = n - 2 * QK_ROPE
    bm = _blk(m, 512, 8)
    return pl.pallas_call(
        functools.partial(_kv_kernel, lora=lora),
        grid=(m // bm,),
        in_specs=[pl.BlockSpec((bm, k), lambda i: (i, 0)),
                  pl.BlockSpec((k, n), lambda i: (0, 0)),
                  pl.BlockSpec((1, lora), lambda i: (0, 0)),
                  pl.BlockSpec((bm, 2 * QK_ROPE), lambda i: (i, 0))],
        out_specs=[pl.BlockSpec((bm, lora), lambda i: (i, 0)),
                   pl.BlockSpec((bm, lora), lambda i: (i, 0)),
                   pl.BlockSpec((bm, QK_ROPE), lambda i: (i, 0)),
                   pl.BlockSpec((bm, LANES), lambda i: (i, 0))],
        out_shape=[jax.ShapeDtypeStruct((m, lora), F32), jax.ShapeDtypeStruct((m, lora), BF16),
                   jax.ShapeDtypeStruct((m, QK_ROPE), F32), jax.ShapeDtypeStruct((m, LANES), BF16)],
        compiler_params=_params("parallel"),
        name="mla_shared_kv",
    )(xb, w, g, cs)


def _uq_kernel(x_ref, wn_ref, wr_ref, cos_ref, sin_ref, o_ref, *, hb, scale):
    x = x_ref[...]
    yn = jnp.dot(x, wn_ref[...].astype(BF16), preferred_element_type=F32)
    yr = jnp.dot(x, wr_ref[...].astype(BF16), preferred_element_type=F32)
    cos, sin = cos_ref[...], sin_ref[...]
    lane = lax.broadcasted_iota(jnp.int32, cos.shape, 1)
    first_half = (lane & (QK_ROPE - 1)) < QK_ROPE // 2
    low = lane < QK_ROPE
    for h in range(hb):
        o_ref[:, h * HEAD_EXT:h * HEAD_EXT + QK_NOPE] = (yn[:, h * QK_NOPE:(h + 1) * QK_NOPE] * scale).astype(BF16)
    for p in range(hb // 2):
        t = yr[:, p * LANES:(p + 1) * LANES]
        sw = jnp.where(first_half, pltpu.roll(t, LANES - QK_ROPE // 2, 1), pltpu.roll(t, QK_ROPE // 2, 1))
        r = (t * cos + sw * sin) * scale
        for j, rr in enumerate((r, pltpu.roll(r, QK_ROPE, 1))):
            lo = (2 * p + j) * HEAD_EXT + QK_NOPE
            o_ref[:, lo:lo + LANES] = jnp.where(low, rr, 0.0).astype(BF16)


def _uq(cq, w_nope, w_rope, cos4, sin4, scale):
    m, k = cq.shape
    heads = w_nope.shape[1] // QK_NOPE
    hb = 4 if heads % 4 == 0 else 2
    assert heads % hb == 0
    bm = _blk(m, 1024, 8)
    return pl.pallas_call(
        functools.partial(_uq_kernel, hb=hb, scale=scale),
        grid=(m // bm, heads // hb),
        in_specs=[pl.BlockSpec((bm, k), lambda i, j: (i, 0)),
                  pl.BlockSpec((k, hb * QK_NOPE), lambda i, j: (0, j)),
                  pl.BlockSpec((k, hb * QK_ROPE), lambda i, j: (0, j)),
                  pl.BlockSpec((bm, LANES), lambda i, j: (i, 0)),
                  pl.BlockSpec((bm, LANES), lambda i, j: (i, 0))],
        out_specs=pl.BlockSpec((bm, hb * HEAD_EXT), lambda i, j: (i, j)),
        out_shape=jax.ShapeDtypeStruct((m, heads * HEAD_EXT), BF16),
        compiler_params=_params("parallel", "arbitrary"),
        name="mla_uq",
    )(cq, w_nope, w_rope, cos4, sin4)


def _prompt_attn_kernel(q_ref, c_ref, krz_ref, wuk_ref, wuvt_ref, o_ref, kext_ref, vt_ref, *, s, tq, hp):
    c = c_ref[...]
    kn = jnp.dot(c, wuk_ref[...], preferred_element_type=F32).astype(BF16)
    krz = krz_ref[...]
    for h in range(hp):
        kext_ref[h, :, :QK_NOPE] = kn[:, h * QK_NOPE:(h + 1) * QK_NOPE]
        kext_ref[h, :, QK_NOPE:] = krz
    vt_ref[...] = lax.dot_general(wuvt_ref[...], c, _NT, preferred_element_type=F32).astype(BF16)
    for h in range(hp):
        for qi in range(s // tq):
            q = q_ref[qi * tq:(qi + 1) * tq, h * HEAD_EXT:(h + 1) * HEAD_EXT]
            m = jnp.full((1, tq), -jnp.inf, F32)
            l = jnp.zeros((1, tq), F32)
            acc = jnp.zeros((V_DIM, tq), F32)
            for kj in range(qi + 1):
                k = kext_ref[h, kj * tq:(kj + 1) * tq, :]
                st = lax.dot_general(k, q, _NT, preferred_element_type=F32)
                if kj == qi:
                    key = lax.broadcasted_iota(jnp.int32, (tq, tq), 0)
                    qry = lax.broadcasted_iota(jnp.int32, (tq, tq), 1)
                    st = jnp.where(key <= qry, st, -jnp.inf)
                m_new = jnp.maximum(m, jnp.max(st, axis=0, keepdims=True))
                p = jnp.exp2(st - m_new)
                a = jnp.exp2(m - m_new)
                l = a * l + jnp.sum(p, axis=0, keepdims=True)
                acc = a * acc + jnp.dot(vt_ref[h * V_DIM:(h + 1) * V_DIM, kj * tq:(kj + 1) * tq],
                                        p.astype(BF16), preferred_element_type=F32)
                m = m_new
            o_ref[qi * tq:(qi + 1) * tq, h * V_DIM:(h + 1) * V_DIM] = (acc / l).T.astype(BF16)


def _prompt_attn(q_ext, cb, krz, wuk2d, wuvt, nb, s):
    h = wuk2d.shape[1] // QK_NOPE
    lora = cb.shape[1]
    tq = _blk(s, 1024)
    hp = 2 if h % 2 == 0 else 1
    return pl.pallas_call(
        functools.partial(_prompt_attn_kernel, s=s, tq=tq, hp=hp),
        grid=(nb, h // hp),
        in_specs=[pl.BlockSpec((s, hp * HEAD_EXT), lambda b, hh: (b, hh)),
                  pl.BlockSpec((s, lora), lambda b, hh: (b, 0)),
                  pl.BlockSpec((s, LANES), lambda b, hh: (b, 0)),
                  pl.BlockSpec((lora, hp * QK_NOPE), lambda b, hh: (0, hh)),
                  pl.BlockSpec((hp * V_DIM, lora), lambda b, hh: (hh, 0))],
        out_specs=pl.BlockSpec((s, hp * V_DIM), lambda b, hh: (b, hh)),
        out_shape=jax.ShapeDtypeStruct((nb * s, h * V_DIM), BF16),
        scratch_shapes=[pltpu.VMEM((hp, s, HEAD_EXT), BF16), pltpu.VMEM((hp * V_DIM, s), BF16)],
        compiler_params=_params("parallel", "arbitrary"),
        name="mla_prompt_attn",
    )(q_ext, cb, krz, wuk2d, wuvt)


def _qlat_kernel(q_ref, wuk_ref, ql_ref, qr_ref, *, heads, tok):
    bb = ql_ref.shape[0]
    for hp in range(heads // 2):
        ql, qr = [], []
        for h in (2 * hp, 2 * hp + 1):
            qn = q_ref[:, h * HEAD_EXT:h * HEAD_EXT + QK_NOPE]
            y = lax.dot_general(qn, wuk_ref[:, h * QK_NOPE:(h + 1) * QK_NOPE], _NT,
                                preferred_element_type=F32)
            ql.append(y.reshape(bb, tok, y.shape[1]))
            r = q_ref[:, h * HEAD_EXT + QK_NOPE:(h + 1) * HEAD_EXT].astype(F32)
            qr.append(r.reshape(bb, tok, LANES))
        lo = 2 * hp * tok
        ql_ref[:, lo:lo + 2 * tok, :] = jnp.concatenate(ql, axis=1).astype(BF16)
        qr_ref[:, lo:lo + 2 * tok, :] = jnp.concatenate(qr, axis=1).astype(BF16)


def _q_latent(q_ext, wuk2d, bd, tok):
    lora = wuk2d.shape[0]
    heads = wuk2d.shape[1] // QK_NOPE
    assert heads % 2 == 0 and tok == SUBLANES
    bb = _blk(bd, 16, 1)
    return pl.pallas_call(
        functools.partial(_qlat_kernel, heads=heads, tok=tok),
        grid=(bd // bb,),
        in_specs=[pl.BlockSpec((bb * tok, heads * HEAD_EXT), lambda i: (i, 0)),
                  pl.BlockSpec((lora, heads * QK_NOPE), lambda i: (0, 0))],
        out_specs=[pl.BlockSpec((bb, heads * tok, lora), lambda i: (i, 0, 0)),
                   pl.BlockSpec((bb, heads * tok, LANES), lambda i: (i, 0, 0))],
        out_shape=[jax.ShapeDtypeStruct((bd, heads * tok, lora), BF16),
                   jax.ShapeDtypeStruct((bd, heads * tok, LANES), BF16)],
        compiler_params=_params("parallel"),
        name="mla_q_latent",
    )(q_ext, wuk2d)


def _sample_attn_kernel(pt_ref, ql_ref, qr_ref, cn_ref, rnt_ref, ckv_hbm, krt_hbm, o_ref,
                        cbuf, rbuf, kc_ref, krt_ref, s_ref, m_ref, l_ref, acc_ref, sem_c, sem_r,
                        *, pg, nsteps, ps, tok, tks):
    step = pl.program_id(1)
    n = pl.program_id(0) * nsteps + step
    total = pl.num_programs(0) * nsteps
    slot = n % 2
    ql = ql_ref[0]
    qr = qr_ref[0]
    rows = ql.shape[0]

    def page_copies(page, slot_, i):
        return (pltpu.make_async_copy(ckv_hbm.at[page], cbuf.at[slot_, i], sem_c.at[slot_]),
                pltpu.make_async_copy(krt_hbm.at[page], rbuf.at[slot_, i], sem_r.at[slot_]))

    def start_group(n_, slot_):
        for i in range(pg):
            for cp in page_copies(pt_ref[n_ * pg + i], slot_, i):
                cp.start()

    @pl.when(n == 0)
    def _():
        start_group(0, 0)

    for i in range(pg):
        for cp in page_copies(0, slot, i):
            cp.wait()

    @pl.when(n + 1 < total)
    def _():
        start_group(n + 1, 1 - slot)

    @pl.when(step == 0)
    def _():
        cn = cn_ref[0]
        s = (lax.dot_general(ql, cn, _NT, preferred_element_type=F32)
             + jnp.dot(qr, rnt_ref[0], preferred_element_type=F32))
        nk = s.shape[1]
        row = lax.broadcasted_iota(jnp.int32, (rows, nk), 0)
        col = lax.broadcasted_iota(jnp.int32, (rows, nk), 1)
        s = jnp.where(col <= (row & (tok - 1)), s, -jnp.inf)
        m = jnp.max(s, axis=1, keepdims=True)
        p = jnp.exp2(s - m)
        m_ref[...] = jnp.broadcast_to(m, m_ref.shape)
        l_ref[...] = jnp.broadcast_to(jnp.sum(p, axis=1, keepdims=True), l_ref.shape)
        acc_ref[...] = jnp.dot(p.astype(BF16), cn, preferred_element_type=F32)
        krt_ref[QK_ROPE:, :] = jnp.zeros((LANES - QK_ROPE, krt_ref.shape[1]), BF16)

    nch = pg * ps // tks
    ppc = tks // ps

    def scores(c):
        for i in range(c * ppc, (c + 1) * ppc):
            kc_ref[i * ps:(i + 1) * ps, :] = cbuf[slot, i].astype(BF16)
            krt_ref[:QK_ROPE, i * ps:(i + 1) * ps] = rbuf[slot, i].astype(BF16)
        kc = kc_ref[c * tks:(c + 1) * tks, :]
        s_ref[c % 2] = (lax.dot_general(ql, kc, _NT, preferred_element_type=F32)
                        + jnp.dot(qr, krt_ref[:, c * tks:(c + 1) * tks], preferred_element_type=F32))

    def softmax_pv(c):
        s = s_ref[c % 2]
        m_prev = m_ref[...]
        m_new = jnp.maximum(m_prev, jnp.max(s, axis=1, keepdims=True))
        a = jnp.exp2(m_prev - m_new)
        p = jnp.exp2(s - m_new[:, :1])
        l_ref[...] = a * l_ref[...] + jnp.sum(p, axis=1, keepdims=True)
        acc_ref[...] = acc_ref[...] * a[:, :1] + jnp.dot(p.astype(BF16), kc_ref[c * tks:(c + 1) * tks, :],
                                                          preferred_element_type=F32)
        m_ref[...] = m_new

    scores(0)
    for c in range(nch):
        if c + 1 < nch:
            scores(c + 1)
        softmax_pv(c)

    @pl.when(step == nsteps - 1)
    def _():
        o_ref[0] = (acc_ref[...] / l_ref[...][:, :1]).astype(BF16)


def _sample_attn(page_table, ql3, qr3, cn_pad, rnt_pad, cache_ckv, cache_krt, tok):
    bd, rows, lora = ql3.shape
    npages = page_table.shape[1]
    ps = cache_ckv.shape[1]
    pg = _blk(npages, 32, 1)
    nsteps = npages // pg
    tks = _blk(pg * ps, 2048)
    nk = cn_pad.shape[1]
    assert tok & (tok - 1) == 0
    pt_flat = page_table.reshape(-1)
    in_specs = [pl.BlockSpec((1, rows, lora), lambda b, st, pt: (b, 0, 0)),
                pl.BlockSpec((1, rows, LANES), lambda b, st, pt: (b, 0, 0)),
                pl.BlockSpec((1, nk, lora), lambda b, st, pt: (b, 0, 0)),
                pl.BlockSpec((1, LANES, nk), lambda b, st, pt: (b, 0, 0)),
                pl.BlockSpec(memory_space=pl.ANY),
                pl.BlockSpec(memory_space=pl.ANY)]
    kern = functools.partial(_sample_attn_kernel, pg=pg, nsteps=nsteps, ps=ps, tok=tok, tks=tks)
    return pl.pallas_call(
        kern,
        grid_spec=pltpu.PrefetchScalarGridSpec(
            num_scalar_prefetch=1,
            grid=(bd, nsteps),
            in_specs=in_specs,
            out_specs=pl.BlockSpec((1, rows, lora), lambda b, st, pt: (b, 0, 0)),
            scratch_shapes=[pltpu.VMEM((2, pg, ps, lora), F32), pltpu.VMEM((2, pg, QK_ROPE, ps), F32),
                            pltpu.VMEM((pg * ps, lora), BF16), pltpu.VMEM((LANES, pg * ps), BF16),
                            pltpu.VMEM((2, rows, tks), F32),
                            pltpu.VMEM((rows, LANES), F32), pltpu.VMEM((rows, LANES), F32),
                            pltpu.VMEM((rows, lora), F32),
                            pltpu.SemaphoreType.DMA((2,)), pltpu.SemaphoreType.DMA((2,))]),
        out_shape=jax.ShapeDtypeStruct((bd, rows, lora), BF16),
        compiler_params=_params("arbitrary", "arbitrary"),
        name="mla_sample_attn",
    )(pt_flat, ql3, qr3, cn_pad, rnt_pad, cache_ckv, cache_krt)


def _uv_kernel(o_ref, w_ref, y_ref, *, heads, tok):
    bb = o_ref.shape[0]
    for hp in range(heads // 2):
        x = o_ref[:, 2 * hp * tok:2 * (hp + 1) * tok, :]
        y = jnp.dot(x.reshape(bb * 2 * tok, x.shape[2]), w_ref[:, 2 * hp * V_DIM:2 * (hp + 1) * V_DIM],
                    preferred_element_type=F32).reshape(bb, 2 * tok, 2 * V_DIM)
        lo = 2 * hp * V_DIM
        y_ref[:, lo:lo + V_DIM] = y[:, :tok, :V_DIM].reshape(bb * tok, V_DIM).astype(BF16)
        y_ref[:, lo + V_DIM:lo + 2 * V_DIM] = y[:, tok:, V_DIM:].reshape(bb * tok, V_DIM).astype(BF16)


def _latent_to_v(o_lat, wuv2d, tok):
    bd, rows, lora = o_lat.shape
    heads = rows // tok
    assert heads % 2 == 0 and tok == SUBLANES
    bb = _blk(bd, 16, 1)
    return pl.pallas_call(
        functools.partial(_uv_kernel, heads=heads, tok=tok),
        grid=(bd // bb,),
        in_specs=[pl.BlockSpec((bb, rows, lora), lambda i: (i, 0, 0)),
                  pl.BlockSpec((lora, heads * V_DIM), lambda i: (0, 0))],
        out_specs=pl.BlockSpec((bb * tok, heads * V_DIM), lambda i: (i, 0)),
        out_shape=jax.ShapeDtypeStruct((bd * tok, heads * V_DIM), BF16),
        compiler_params=_params("parallel"),
        name="mla_latent_to_v",
    )(o_lat, wuv2d)


def _rope_tables(pos):
    half = QK_ROPE // 2
    inv = 1.0 / (ROPE_THETA ** (jnp.arange(half, dtype=F32) / half))
    ang = pos.astype(F32)[:, None] * inv[None, :]
    cos, sin = jnp.cos(ang), jnp.sin(ang)
    return (jnp.concatenate([cos, cos, -sin, sin], axis=1),
            jnp.concatenate([cos, cos, cos, cos], axis=1),
            jnp.concatenate([-sin, sin, -sin, sin], axis=1))


def _swap_halves(w):
    half = QK_ROPE // 2
    return jnp.concatenate([w[..., half:], w[..., :half]], axis=-1)


def kernel(x_prompt, x_sample, state_conv, cache_ckv, cache_krope, page_table, ln_g, ln_b, conv_w_pw1, conv_b_pw1, conv_w_dw, conv_b_dw, conv_ln_g, conv_ln_b, conv_w_pw2, conv_b_pw2, mla_w_dq, mla_q_norm, mla_w_uq, mla_w_o, kv_w_dkv, kv_norm, kv_w_kr, kv_w_uk, kv_w_uv, ffn_w_gate, ffn_w_up, ffn_w_down):
    nb, s, d = x_prompt.shape
    bd, t, _ = x_sample.shape
    depth = ln_g.shape[0]
    assert depth == 2 and conv_w_pw1.shape[0] == 1 and mla_w_dq.shape[0] == 1
    heads = mla_w_uq.shape[2]
    lora = kv_w_dkv.shape[1]
    ps = cache_ckv.shape[1]
    past = page_table.shape[1] * ps
    kw = conv_w_dw.shape[1]
    alpha = (2.0 * depth) ** 0.25
    q_scale = float(QK_NOPE + QK_ROPE) ** -0.5 * math.log2(math.e)

    def vec(v):
        return v.reshape(1, -1)

    w_pw1 = conv_w_pw1[0]
    w_pw2 = conv_w_pw2
    wd3 = ffn_w_down.astype(BF16)
    w_kv = jnp.concatenate([kv_w_dkv, kv_w_kr, _swap_halves(kv_w_kr)], axis=1).astype(BF16)
    w_dq = mla_w_dq[0].astype(BF16)
    w_uq = mla_w_uq[0]
    w_uq_nope = w_uq[..., :QK_NOPE].reshape(w_uq.shape[0], heads * QK_NOPE)
    w_uq_rope = w_uq[..., QK_NOPE:].reshape(w_uq.shape[0], heads * QK_ROPE)
    wuk2d = kv_w_uk.reshape(lora, heads * QK_NOPE).astype(BF16)
    wuv2d = kv_w_uv.reshape(lora, heads * V_DIM).astype(BF16)
    wuvt = jnp.transpose(kv_w_uv, (1, 2, 0)).reshape(heads * V_DIM, lora).astype(BF16)
    w_o3 = mla_w_o.reshape(1, heads * V_DIM, d)
    conv_args = (conv_w_dw[0], vec(conv_b_dw[0]), vec(conv_ln_g[0]), vec(conv_ln_b[0]))

    def ffn(h, hb, l):
        mid = _gate_up(hb, ffn_w_gate, ffn_w_up, l)
        pre = _mm_residual(mid, wd3, l, h, None, alpha, ffn_w_down.shape[1] // 2)
        return _ln(pre, vec(ln_g[l, 1]), vec(ln_b[l, 1]))

    def layer0(x, conv):
        g = _pw1_glu(x, w_pw1, vec(conv_b_pw1[0]))
        z, extra = conv(g)
        pre = _mm_residual(z, w_pw2, 0, x, vec(conv_b_pw2[0]), alpha, d)
        h1, h1b = _ln(pre, vec(ln_g[0, 0]), vec(ln_b[0, 0]))
        h2, h2b = ffn(h1, h1b, 0)
        return h2, h2b, extra

    def qkv(h2b, pos, reps):
        cs, cos4, sin4 = (jnp.tile(tb, (reps, 1)) for tb in _rope_tables(pos))
        c, cb, kr, krz = _shared_kv(h2b, w_kv, vec(kv_norm), cs)
        cq = _dq(h2b, w_dq, vec(mla_q_norm[0]))
        return c, cb, kr, krz, _uq(cq, w_uq_nope, w_uq_rope, cos4, sin4, q_scale)

    def layer1_tail(o, h2):
        pre = _mm_residual(o, w_o3, 0, h2, None, alpha, 4096)
        h3, h3b = _ln(pre, vec(ln_g[1, 0]), vec(ln_b[1, 0]))
        return ffn(h3, h3b, 1)[0]

    def conv_p(g):
        g3 = g.reshape(nb, s, d)
        return _conv_prompt(g3, *conv_args).reshape(nb * s, d), g3[:, s - (kw - 1):]

    h2, h2b, conv_prompt = layer0(x_prompt.reshape(nb * s, d), conv_p)
    ckv_p, cb_p, kr_p, krz_p, q_p = qkv(h2b, jnp.arange(s), nb)
    o_p = _prompt_attn(q_p, cb_p, krz_p, wuk2d, wuvt, nb, s)
    y_p = layer1_tail(o_p, h2)

    def conv_s(g):
        z, new_state = _conv_sample(state_conv[0], g.reshape(bd, t, d), *conv_args)
        return z.reshape(bd * t, d), new_state

    g2, g2b, conv_sample = layer0(x_sample.reshape(bd * t, d), conv_s)
    ckv_s, cb_s, kr_s, krz_s, q_s = qkv(g2b, past + jnp.arange(t), bd)
    ql3, qr3 = _q_latent(q_s, wuk2d, bd, t)
    nk = LANES
    cn_pad = jnp.pad(cb_s.reshape(bd, t, lora), ((0, 0), (0, nk - t), (0, 0)))
    rnt_pad = jnp.pad(jnp.swapaxes(krz_s.reshape(bd, t, LANES), 1, 2), ((0, 0), (0, 0), (0, nk - t)))
    cache_krt = jnp.swapaxes(cache_krope, 1, 2)
    o_lat = _sample_attn(page_table, ql3, qr3, cn_pad, rnt_pad, cache_ckv, cache_krt, t)
    o_s = _latent_to_v(o_lat, wuv2d, t)
    y_s = layer1_tail(o_s, g2)

    return (y_p.reshape(nb, s, d), y_s.reshape(bd, t, d), conv_prompt[None], conv_sample[None],
            ckv_p.reshape(nb, s, lora), ckv_s.reshape(bd, t, lora),
            kr_p.reshape(nb, s, QK_ROPE), kr_s.reshape(bd, t, QK_ROPE))
```
